```python
import jax, jax.numpy as jnp
from jax import lax
import numpy as np

D_MODEL = 2048
BATCH = 1
SEQ = 8192
DEPTH = 4

MEM_LEN = 256
CONV_DIM = D_MODEL // 2
CONV_KERNEL = 31
SCONV_DIM = D_MODEL // 2
SCONV_KERNEL = 3
XATTN_DIM = D_MODEL // 2
XATTN_HEADS = 4
XATTN_HEAD_DIM = XATTN_DIM // XATTN_HEADS
N_BRANCH = 3
D_FF = 4 * D_MODEL
EPS = 1e-6

IN_SIZES = (CONV_DIM, CONV_DIM, SCONV_DIM, SCONV_DIM, SCONV_DIM, XATTN_DIM, N_BRANCH * D_MODEL)
IN_DIM = int(sum(IN_SIZES))
IN_SPLITS = tuple(int(v) for v in np.cumsum(IN_SIZES)[:-1])

kernel_name = "hybrid_conformer_shortconv_memxattn_block"


def rms_norm(x, g):
    xf = x.astype(jnp.float32)
    y = xf * lax.rsqrt(jnp.mean(xf * xf, axis=-1, keepdims=True) + EPS)
    return (y * g.astype(jnp.float32)).astype(x.dtype)


def layer_norm(x, g, b):
    xf = x.astype(jnp.float32)
    mu = jnp.mean(xf, axis=-1, keepdims=True)
    var = jnp.mean(jnp.square(xf - mu), axis=-1, keepdims=True)
    y = (xf - mu) * lax.rsqrt(var + EPS)
    return (y * g.astype(jnp.float32) + b.astype(jnp.float32)).astype(x.dtype)


def causal_depthwise_conv(u, w):
    k, c = w.shape
    return lax.conv_general_dilated(
        u, w[:, None, :].astype(u.dtype),
        window_strides=(1,), padding=[(k - 1, 0)],
        dimension_numbers=("NWC", "WIO", "NWC"),
        feature_group_count=c)


def memory_cross_attention(q, mem_n, w_kv):
    b, s, _ = q.shape
    m = mem_n.shape[1]
    kv = mem_n @ w_kv
    k, v = jnp.split(kv, 2, axis=-1)
    qh = q.reshape(b, s, XATTN_HEADS, XATTN_HEAD_DIM)
    kh = k.reshape(b, m, XATTN_HEADS, XATTN_HEAD_DIM)
    vh = v.reshape(b, m, XATTN_HEADS, XATTN_HEAD_DIM)
    scale = XATTN_HEAD_DIM ** -0.5
    scores = jnp.einsum("bshd,bmhd->bhsm", qh, kh).astype(jnp.float32) * scale
    probs = jax.nn.softmax(scores, axis=-1).astype(v.dtype)
    o = jnp.einsum("bhsm,bmhd->bshd", probs, vh)
    return o.reshape(b, s, XATTN_DIM)


def hybrid_layer(x, mem, g_mix_pre, w_in, conv_a_w, conv_a_b, ln_a_g, ln_a_b, w_a_out,
                 conv_b_w, w_b_out, g_mem, w_kv, w_x_out, w_o, g_mix_post,
                 g_mlp_pre, w_up, w_down, g_mlp_post):
    b, s, d = x.shape
    h = rms_norm(x, g_mix_pre)
    proj = h @ w_in
    a_val, a_gate, sb, sc, sx, q, gates = jnp.split(proj, IN_SPLITS, axis=-1)

    a = a_val * jax.nn.sigmoid(a_gate)
    a = causal_depthwise_conv(a, conv_a_w) + conv_a_b
    a = jax.nn.silu(layer_norm(a, ln_a_g, ln_a_b))
    y_a = a @ w_a_out

    u = causal_depthwise_conv(sc * sx, conv_b_w)
    y_b = (sb * u) @ w_b_out

    mem_n = rms_norm(mem, g_mem)
    y_x = memory_cross_attention(q, mem_n, w_kv) @ w_x_out

    g = jax.nn.sigmoid(gates).reshape(b, s, N_BRANCH, d)
    merged = g[:, :, 0] * y_a + g[:, :, 1] * y_b + g[:, :, 2] * y_x
    x = x + rms_norm(merged @ w_o, g_mix_post)

    h = rms_norm(x, g_mlp_pre)
    f = jnp.square(jax.nn.relu(h @ w_up)) @ w_down
    x = x + rms_norm(f, g_mlp_post)
    return x


def setup_inputs(seed: int = 0) -> dict:
    key = jax.random.key(seed)
    ks = jax.random.split(key, 24)
    f32 = jnp.float32

    def nrm(k, shape, scale):
        return jax.random.normal(k, shape, f32) * scale

    def gain(k, shape):
        return 1.0 + 0.02 * jax.random.normal(k, shape, f32)

    L, D = DEPTH, D_MODEL
    return {
        "x": nrm(ks[0], (BATCH, SEQ, D), 1.0),
        "mem": nrm(ks[1], (BATCH, MEM_LEN, D), 1.0),
        "g_mix_pre": gain(ks[2], (L, D)),
        "w_in": nrm(ks[3], (L, D, IN_DIM), D ** -0.5),
        "conv_a_w": nrm(ks[4], (L, CONV_KERNEL, CONV_DIM), CONV_KERNEL ** -0.5),
        "conv_a_b": nrm(ks[5], (L, CONV_DIM), 0.02),
        "ln_a_g": gain(ks[6], (L, CONV_DIM)),
        "ln_a_b": nrm(ks[7], (L, CONV_DIM), 0.02),
        "w_a_out": nrm(ks[8], (L, CONV_DIM, D), CONV_DIM ** -0.5),
        "conv_b_w": nrm(ks[9], (L, SCONV_KERNEL, SCONV_DIM), SCONV_KERNEL ** -0.5),
        "w_b_out": nrm(ks[10], (L, SCONV_DIM, D), SCONV_DIM ** -0.5),
        "g_mem": gain(ks[11], (L, D)),
        "w_kv": nrm(ks[12], (L, D, 2 * XATTN_DIM), D ** -0.5),
        "w_x_out": nrm(ks[13], (L, XATTN_DIM, D), XATTN_DIM ** -0.5),
        "w_o": nrm(ks[14], (L, D, D), D ** -0.5),
        "g_mix_post": gain(ks[15], (L, D)),
        "g_mlp_pre": gain(ks[16], (L, D)),
        "w_up": nrm(ks[17], (L, D, D_FF), D ** -0.5),
        "w_down": nrm(ks[18], (L, D_FF, D), D_FF ** -0.5),
        "g_mlp_post": gain(ks[19], (L, D)),
    }


def reference(x, mem, g_mix_pre, w_in, conv_a_w, conv_a_b, ln_a_g, ln_a_b, w_a_out,
              conv_b_w, w_b_out, g_mem, w_kv, w_x_out, w_o, g_mix_post,
              g_mlp_pre, w_up, w_down, g_mlp_post):
    for l in range(DEPTH):
        x = hybrid_layer(x, mem, g_mix_pre[l], w_in[l], conv_a_w[l], conv_a_b[l],
                         ln_a_g[l], ln_a_b[l], w_a_out[l], conv_b_w[l], w_b_out[l],
                         g_mem[l], w_kv[l], w_x_out[l], w_o[l], g_mix_post[l],
                         g_mlp_pre[l], w_up[l], w_down[l], g_mlp_post[l])
    return x
```

```python
import functools

import jax
import jax.numpy as jnp
from jax import lax
from jax.experimental import pallas as pl
from jax.experimental.pallas import tpu as pltpu

F32 = jnp.float32
BF16 = jnp.bfloat16

EPS = 1e-6
XATTN_HEADS = 4
N_BRANCH = 3

V7X_VMEM_BYTES = 64 * 1024 * 1024
V7X_LANES = 128
V7X_SUBLANES = 8
BF16_ROWS_PER_VREG = 16

HALO_ROWS = 32
NORM_CHUNK_ROWS = 32
CONV_CHUNK_ROWS = 64
CONV_CHUNK_COLS = 256


def _vmem_limit(estimate_bytes):
    return int(min(V7X_VMEM_BYTES - 4 * 1024 * 1024, estimate_bytes * 5 // 4))


def _rms_rows(x, g):
    ms = jnp.mean(x * x, axis=-1, keepdims=True)
    return x * lax.rsqrt(ms + EPS) * g


def _for_row_chunks(n_rows, chunk, body):
    def step(c, carry):
        body(pl.ds(pl.multiple_of(c * chunk, chunk), chunk))
        return carry
    lax.fori_loop(0, n_rows // chunk, step, 0)


def _kv_kernel(mem_ref, g_ref, w_ref, kt_ref, v_ref, *, xdim, scale):
    mem_n = _rms_rows(mem_ref[...], g_ref[...]).astype(BF16)
    kv = jnp.dot(mem_n, w_ref[...], preferred_element_type=F32)
    kt_ref[...] = (kv[:, :xdim] * scale).T.astype(BF16)
    v_ref[...] = kv[:, xdim:].astype(BF16)


def _memory_kv(mem, g_mem, w_kv_bf):
    depth, d, two_x = w_kv_bf.shape
    xdim = two_x // 2
    m = mem.shape[0]
    head_dim = xdim // XATTN_HEADS
    est = 2 * (m * d * 4 + d * two_x * 2 + 2 * m * xdim * 2) + 3 * m * two_x * 4
    return pl.pallas_call(
        functools.partial(_kv_kernel, xdim=xdim, scale=head_dim ** -0.5),
        grid=(depth,),
        in_specs=[
            pl.BlockSpec((m, d), lambda l: (0, 0)),
            pl.BlockSpec((None, 1, d), lambda l: (l, 0, 0)),
            pl.BlockSpec((None, d, two_x), lambda l: (l, 0, 0)),
        ],
        out_specs=[
            pl.BlockSpec((None, xdim, m), lambda l: (l, 0, 0)),
            pl.BlockSpec((None, m, xdim), lambda l: (l, 0, 0)),
        ],
        out_shape=[
            jax.ShapeDtypeStruct((depth, xdim, m), BF16),
            jax.ShapeDtypeStruct((depth, m, xdim), BF16),
        ],
        compiler_params=pltpu.CompilerParams(
            dimension_semantics=("arbitrary",), vmem_limit_bytes=_vmem_limit(est)),
        name="memory_kv",
    )(mem, g_mem, w_kv_bf)


def _in_proj_kernel(x_ref, g_ref, w_ref, o_ref, h_ref):
    @pl.when(pl.program_id(1) == 0)
    def _():
        def norm(rows):
            h_ref[rows, :] = _rms_rows(x_ref[rows, :], g_ref[...]).astype(BF16)
        _for_row_chunks(x_ref.shape[0], NORM_CHUNK_ROWS, norm)

    o_ref[...] = jnp.dot(h_ref[...], w_ref[...],
                         preferred_element_type=F32).astype(o_ref.dtype)


def _in_proj(x, g_pre, w_in_bf, layer, *, tm, tn):
    s, d = x.shape
    n = w_in_bf.shape[2]
    est = 2 * tm * d * 4 + tm * d * 2 + 2 * d * tn * 2 + 2 * tm * tn * 2 + tm * tn * 4
    return pl.pallas_call(
        _in_proj_kernel,
        grid=(s // tm, n // tn),
        in_specs=[
            pl.BlockSpec((tm, d), lambda i, j: (i, 0)),
            pl.BlockSpec((None, 1, d), lambda i, j: (layer, 0, 0)),
            pl.BlockSpec((None, d, tn), lambda i, j: (layer, 0, j)),
        ],
        out_specs=pl.BlockSpec((tm, tn), lambda i, j: (i, j)),
        out_shape=jax.ShapeDtypeStruct((s, n), BF16),
        scratch_shapes=[pltpu.VMEM((tm, d), BF16)],
        compiler_params=pltpu.CompilerParams(
            dimension_semantics=("parallel", "arbitrary"),
            vmem_limit_bytes=_vmem_limit(est)),
        name="in_proj",
    )(x, g_pre, w_in_bf)


def _causal_conv_chunks(src_ref, w_ref, n_taps, tm, emit):
    base = HALO_ROWS - (n_taps - 1)
    n_col_chunks = src_ref.shape[1] // CONV_CHUNK_COLS
    for r0 in range(0, tm, CONV_CHUNK_ROWS):
        def col_step(c, carry, r0=r0):
            cols = pl.ds(pl.multiple_of(c * CONV_CHUNK_COLS, CONV_CHUNK_COLS), CONV_CHUNK_COLS)
            acc = None
            for k in range(n_taps):
                term = src_ref[pl.ds(base + r0 + k, CONV_CHUNK_ROWS), cols] * w_ref[pl.ds(k, 1), cols]
                acc = term if acc is None else acc + term
            emit(pl.ds(r0, CONV_CHUNK_ROWS), cols, acc)
            return carry
        lax.fori_loop(0, n_col_chunks, col_step, 0)


def _mix_kernel(av_ref, ag_ref, sb_ref, sc_ref, sx_ref, q_ref, gates_ref,
                hav_ref, hag_ref, hsc_ref, hsx_ref, x_ref, kt_ref, v_ref,
                caw_ref, cab_ref, lng_ref, lnb_ref, wa_ref, cbw_ref, wb_ref, wx_ref,
                wo_ref, gpost_ref, o_ref,
                a_ext, p_ext, conv_a, act_a, act_b, act_x, *, tm, d, n_taps_a, n_taps_b):
    not_first = pl.program_id(0) > 0
    tile = pl.ds(HALO_ROWS, tm)
    halo = pl.ds(0, HALO_ROWS)

    a_halo = hav_ref[...].astype(F32) * jax.nn.sigmoid(hag_ref[...].astype(F32))
    a_ext[halo, :] = jnp.where(not_first, a_halo, 0.0)
    p_halo = hsc_ref[...].astype(F32) * hsx_ref[...].astype(F32)
    p_ext[halo, :] = jnp.where(not_first, p_halo, 0.0)

    def fill(rows):
        dst = pl.ds(pl.multiple_of(HALO_ROWS + rows.start, NORM_CHUNK_ROWS), rows.size)
        a_ext[dst, :] = av_ref[rows, :].astype(F32) * jax.nn.sigmoid(ag_ref[rows, :].astype(F32))
        p_ext[dst, :] = sc_ref[rows, :].astype(F32) * sx_ref[rows, :].astype(F32)
    _for_row_chunks(tm, NORM_CHUNK_ROWS, fill)

    def emit_a(rows, cols, acc):
        conv_a[rows, cols] = acc + cab_ref[:, cols]
    _causal_conv_chunks(a_ext, caw_ref, n_taps_a, tm, emit_a)

    def ln_swish(rows):
        c = conv_a[rows, :]
        mu = jnp.mean(c, axis=-1, keepdims=True)
        cen = c - mu
        var = jnp.mean(cen * cen, axis=-1, keepdims=True)
        y = cen * lax.rsqrt(var + EPS) * lng_ref[...] + lnb_ref[...]
        act_a[rows, :] = (y * jax.nn.sigmoid(y)).astype(BF16)
    _for_row_chunks(tm, NORM_CHUNK_ROWS, ln_swish)

    def emit_b(rows, cols, acc):
        act_b[rows, cols] = (sb_ref[rows, cols].astype(F32) * acc).astype(BF16)
    _causal_conv_chunks(p_ext, cbw_ref, n_taps_b, tm, emit_b)

    hd = kt_ref.shape[0] // XATTN_HEADS
    for h in range(XATTN_HEADS):
        hs = pl.ds(h * hd, hd)
        s = jnp.dot(q_ref[:, hs], kt_ref[hs, :], preferred_element_type=F32)
        e = jnp.exp(s - jnp.max(s, axis=-1, keepdims=True))
        p = e * (1.0 / jnp.sum(e, axis=-1, keepdims=True))
        act_x[:, hs] = jnp.dot(p.astype(BF16), v_ref[:, hs],
                               preferred_element_type=F32).astype(BF16)

    merged = None
    for b, (act, w) in enumerate(((act_a, wa_ref), (act_b, wb_ref), (act_x, wx_ref))):
        y = jnp.dot(act[...], w[...], preferred_element_type=F32)
        gate = jax.nn.sigmoid(gates_ref[:, pl.ds(b * d, d)].astype(F32))
        merged = gate * y if merged is None else merged + gate * y
    z = jnp.dot(merged.astype(BF16), wo_ref[...], preferred_element_type=F32)
    o_ref[...] = x_ref[...] + _rms_rows(z, gpost_ref[...])


def _mix(x, proj, kt, v, caw, cab, lng, lnb, wa_bf, cbw, wb_bf, wx_bf, wo_bf, gpost, layer, *, tm):
    s, d = x.shape
    cdim = caw.shape[2]
    m, xdim = v.shape[1], v.shape[2]
    assert xdim == cdim
    n_taps_a, n_taps_b = caw.shape[1], cbw.shape[1]
    assert max(n_taps_a, n_taps_b) - 1 <= HALO_ROWS and tm % HALO_ROWS == 0
    halo_per_tile = tm // HALO_ROWS
    col = dict(av=0, ag=1, sb=2, sc=3, sx=4, q=5)
    gates_block = (6 * cdim) // (N_BRANCH * d)
    assert gates_block * N_BRANCH * d == 6 * cdim

    def tile_spec(c):
        return pl.BlockSpec((tm, cdim), lambda i: (i, c))

    def halo_spec(c):
        return pl.BlockSpec((HALO_ROWS, cdim), lambda i: (jnp.maximum(i * halo_per_tile - 1, 0), c))

    def layer_spec(shape):
        zeros = (0,) * len(shape)
        return pl.BlockSpec((None,) + shape, lambda i: (layer,) + zeros,
                            pipeline_mode=pl.Buffered(1))

    in_specs = [
        tile_spec(col["av"]), tile_spec(col["ag"]), tile_spec(col["sb"]), tile_spec(col["sc"]),
        tile_spec(col["sx"]), tile_spec(col["q"]),
        pl.BlockSpec((tm, N_BRANCH * d), lambda i: (i, gates_block)),
        halo_spec(col["av"]), halo_spec(col["ag"]), halo_spec(col["sc"]), halo_spec(col["sx"]),
        pl.BlockSpec((tm, d), lambda i: (i, 0)),
        layer_spec((xdim, m)), layer_spec((m, xdim)),
        layer_spec((n_taps_a, cdim)), layer_spec((1, cdim)), layer_spec((1, cdim)),
        layer_spec((1, cdim)), layer_spec((cdim, d)),
        layer_spec((n_taps_b, cdim)), layer_spec((cdim, d)), layer_spec((cdim, d)),
        layer_spec((d, d)), layer_spec((1, d)),
    ]
    weights = 3 * cdim * d * 2 + d * d * 2 + 2 * cdim * m * 2
    tiles = 2 * (6 * tm * cdim * 2 + tm * N_BRANCH * d * 2 + 4 * HALO_ROWS * cdim * 2 + 2 * tm * d * 4)
    scratch = 2 * (tm + HALO_ROWS) * cdim * 4 + tm * cdim * 4 + 3 * tm * cdim * 2
    temps = 6 * tm * d * 4
    kernel = functools.partial(_mix_kernel, tm=tm, d=d, n_taps_a=n_taps_a, n_taps_b=n_taps_b)
    return pl.pallas_call(
        kernel,
        grid=(s // tm,),
        in_specs=in_specs,
        out_specs=pl.BlockSpec((tm, d), lambda i: (i, 0)),
        out_shape=jax.ShapeDtypeStruct((s, d), F32),
        scratch_shapes=[
            pltpu.VMEM((tm + HALO_ROWS, cdim), F32),
            pltpu.VMEM((tm + HALO_ROWS, cdim), F32),
            pltpu.VMEM((tm, cdim), F32),
            pltpu.VMEM((tm, cdim), BF16),
            pltpu.VMEM((tm, cdim), BF16),
            pltpu.VMEM((tm, cdim), BF16),
        ],
        compiler_params=pltpu.CompilerParams(
            dimension_semantics=("arbitrary",),
            vmem_limit_bytes=_vmem_limit(weights + tiles + scratch + temps)),
        name="token_mix",
    )(proj, proj, proj, proj, proj, proj, proj, proj, proj, proj, proj, x, kt, v,
      caw, cab, lng, lnb, wa_bf, cbw, wb_bf, wx_bf, wo_bf, gpost)


def _mlp_kernel(x_ref, gpre_ref, wup_ref, wdn_ref, gpost_ref, o_ref, h_ref):
    j = pl.program_id(1)

    @pl.when(j == 0)
    def _():
        def norm(rows):
            h_ref[rows, :] = _rms_rows(x_ref[rows, :], gpre_ref[...]).astype(BF16)
            o_ref[rows, :] = jnp.zeros((rows.size, o_ref.shape[1]), F32)
        _for_row_chunks(x_ref.shape[0], NORM_CHUNK_ROWS, norm)

    up = jnp.dot(h_ref[...], wup_ref[...], preferred_element_type=F32)
    r = jnp.square(jnp.maximum(up, 0.0)).astype(BF16)
    o_ref[...] += jnp.dot(r, wdn_ref[...], preferred_element_type=F32)

    @pl.when(j == pl.num_programs(1) - 1)
    def _():
        def finish(rows):
            o_ref[rows, :] = x_ref[rows, :] + _rms_rows(o_ref[rows, :], gpost_ref[...])
        _for_row_chunks(x_ref.shape[0], NORM_CHUNK_ROWS, finish)


def _mlp(x, g_pre, w_up_bf, w_down_bf, g_post, layer, *, tm, tf):
    s, d = x.shape
    f = w_up_bf.shape[2]
    est = (2 * tm * d * 4 + tm * d * 2 + 2 * d * tf * 2 + 2 * tf * d * 2 + 2 * tm * d * 4
           + tm * tf * 6 + tm * d * 4)
    return pl.pallas_call(
        _mlp_kernel,
        grid=(s // tm, f // tf),
        in_specs=[
            pl.BlockSpec((tm, d), lambda i, j: (i, 0)),
            pl.BlockSpec((None, 1, d), lambda i, j: (layer, 0, 0)),
            pl.BlockSpec((None, d, tf), lambda i, j: (layer, 0, j)),
            pl.BlockSpec((None, tf, d), lambda i, j: (layer, j, 0)),
            pl.BlockSpec((None, 1, d), lambda i, j: (layer, 0, 0)),
        ],
        out_specs=pl.BlockSpec((tm, d), lambda i, j: (i, 0)),
        out_shape=jax.ShapeDtypeStruct((s, d), F32),
        scratch_shapes=[pltpu.VMEM((tm, d), BF16)],
        compiler_params=pltpu.CompilerParams(
            dimension_semantics=("parallel", "arbitrary"),
            vmem_limit_bytes=_vmem_limit(est)),
        name="mlp",
    )(x, g_pre, w_up_bf, w_down_bf, g_post)


def kernel(x, mem, g_mix_pre, w_in, conv_a_w, conv_a_b, ln_a_g, ln_a_b, w_a_out, conv_b_w, w_b_out, g_mem, w_kv, w_x_out, w_o, g_mix_post, g_mlp_pre, w_up, w_down, g_mlp_post):
    batch, s, d = x.shape
    assert batch == 1
    depth = w_in.shape[0]

    def row(p):
        return p.reshape(p.shape[0], 1, p.shape[1])

    bf = lambda w: w.astype(BF16)
    w_in_bf, w_kv_bf, w_up_bf, w_down_bf = bf(w_in), bf(w_kv), bf(w_up), bf(w_down)
    wa_bf, wb_bf, wx_bf, wo_bf = bf(w_a_out), bf(w_b_out), bf(w_x_out), bf(w_o)
    g_mix_pre, g_mix_post, g_mlp_pre, g_mlp_post, g_mem = map(
        row, (g_mix_pre, g_mix_post, g_mlp_pre, g_mlp_post, g_mem))
    conv_a_b, ln_a_g, ln_a_b = map(row, (conv_a_b, ln_a_g, ln_a_b))

    kt, v = _memory_kv(mem[0], g_mem, w_kv_bf)
    xs = x[0]
    for l in range(depth):
        proj = _in_proj(xs, g_mix_pre, w_in_bf, l, tm=1024, tn=1024)
        xs = _mix(xs, proj, kt, v, conv_a_w, conv_a_b, ln_a_g, ln_a_b, wa_bf, conv_b_w,
                  wb_bf, wx_bf, wo_bf, g_mix_post, l, tm=256)
        xs = _mlp(xs, g_mlp_pre, w_up_bf, w_down_bf, g_mlp_post, l, tm=1024, tf=512)
    return xs[None]
```

```python
import functools

import jax
import jax.numpy as jnp
from jax import lax
from jax.experimental import pallas as pl
from jax.experimental.pallas import tpu as pltpu

F32 = jnp.float32
BF16 = jnp.bfloat16

EPS = 1e-6
XATTN_HEADS = 4
N_BRANCH = 3

V7X_VMEM_BYTES = 64 * 1024 * 1024
V7X_LANES = 128
V7X_SUBLANES = 8
BF16_ROWS_PER_VREG = 16

HALO_ROWS = 32
NORM_CHUNK_ROWS = 32
STATS_UNROLL = 8
CONV_CHUNK_ROWS = 64
CONV_CHUNK_COLS = 128


def _vmem_limit(estimate_bytes):
    return int(min(V7X_VMEM_BYTES - 4 * 1024 * 1024, estimate_bytes * 5 // 4))


def _rms_rows(x, g):
    ms = jnp.mean(x * x, axis=-1, keepdims=True)
    return x * lax.rsqrt(ms + EPS) * g


def _for_row_chunks(n_rows, chunk, body, unroll=2):
    def step(c, carry):
        body(pl.ds(pl.multiple_of(c * chunk, chunk), chunk))
        return carry
    lax.fori_loop(0, n_rows // chunk, step, 0, unroll=unroll)


def _rms_tile(src_ref, inv_ref, g_ref, emit):
    n_rows, width = src_ref.shape

    def stats(rows):
        v = src_ref[rows, :]
        inv = lax.rsqrt(jnp.mean(v * v, axis=-1, keepdims=True) + EPS)
        inv_ref[rows, :] = jnp.broadcast_to(inv, (rows.size, V7X_LANES))
    _for_row_chunks(n_rows, NORM_CHUNK_ROWS, stats, unroll=STATS_UNROLL)

    def apply(rows):
        inv = inv_ref[rows, :]
        for c0 in range(0, width, V7X_LANES):
            cols = pl.ds(c0, V7X_LANES)
            emit(rows, cols, src_ref[rows, cols] * inv * g_ref[:, cols])
    _for_row_chunks(n_rows, NORM_CHUNK_ROWS, apply)


def _kv_kernel(mem_ref, g_ref, w_ref, kt_ref, v_ref, *, xdim, scale):
    mem_n = _rms_rows(mem_ref[...], g_ref[...]).astype(BF16)
    kv = jnp.dot(mem_n, w_ref[...].astype(BF16), preferred_element_type=F32)
    kt_ref[...] = (kv[:, :xdim] * scale).T.astype(BF16)
    v_ref[...] = kv[:, xdim:].astype(BF16)


def _memory_kv(mem, g_mem, w_kv):
    depth, d, two_x = w_kv.shape
    xdim = two_x // 2
    m = mem.shape[0]
    head_dim = xdim // XATTN_HEADS
    est = (2 * (m * d * 4 + d * two_x * 4 + 2 * m * xdim * 2) + d * two_x * 2
           + 3 * m * two_x * 4)
    return pl.pallas_call(
        functools.partial(_kv_kernel, xdim=xdim, scale=head_dim ** -0.5),
        grid=(depth,),
        in_specs=[
            pl.BlockSpec((m, d), lambda l: (0, 0)),
            pl.BlockSpec((None, 1, d), lambda l: (l, 0, 0)),
            pl.BlockSpec((None, d, two_x), lambda l: (l, 0, 0)),
        ],
        out_specs=[
            pl.BlockSpec((None, xdim, m), lambda l: (l, 0, 0)),
            pl.BlockSpec((None, m, xdim), lambda l: (l, 0, 0)),
        ],
        out_shape=[
            jax.ShapeDtypeStruct((depth, xdim, m), BF16),
            jax.ShapeDtypeStruct((depth, m, xdim), BF16),
        ],
        compiler_params=pltpu.CompilerParams(
            dimension_semantics=("arbitrary",), vmem_limit_bytes=_vmem_limit(est)),
        name="memory_kv",
    )(mem, g_mem, w_kv)


def _in_proj_kernel(x_ref, g_ref, w_ref, o_ref, h_ref, inv_ref):
    @pl.when(pl.program_id(1) == 0)
    def _():
        def emit(rows, cols, y):
            h_ref[rows, cols] = y.astype(BF16)
        _rms_tile(x_ref, inv_ref, g_ref, emit)

    o_ref[...] = jnp.dot(h_ref[...], w_ref[...].astype(BF16),
                         preferred_element_type=F32).astype(o_ref.dtype)


def _in_proj(x, g_pre, w_in, layer, *, tm, tn):
    s, d = x.shape
    n = w_in.shape[2]
    est = 2 * tm * d * 4 + tm * d * 2 + 2 * d * tn * 4 + d * tn * 2 + 2 * tm * tn * 2
    return pl.pallas_call(
        _in_proj_kernel,
        grid=(s // tm, n // tn),
        in_specs=[
            pl.BlockSpec((tm, d), lambda i, j: (i, 0)),
            pl.BlockSpec((None, 1, d), lambda i, j: (layer, 0, 0)),
            pl.BlockSpec((None, d, tn), lambda i, j: (layer, 0, j)),
        ],
        out_specs=pl.BlockSpec((tm, tn), lambda i, j: (i, j)),
        out_shape=jax.ShapeDtypeStruct((s, n), BF16),
        scratch_shapes=[pltpu.VMEM((tm, d), BF16), pltpu.VMEM((tm, V7X_LANES), F32)],
        compiler_params=pltpu.CompilerParams(
            dimension_semantics=("parallel", "arbitrary"),
            vmem_limit_bytes=_vmem_limit(est)),
        name="in_proj",
    )(x, g_pre, w_in)


def _causal_conv_chunks(src_ref, w_ref, n_taps, tm, emit):
    base = HALO_ROWS - (n_taps - 1)
    ext_rows = CONV_CHUNK_ROWS + HALO_ROWS
    n_col_chunks = src_ref.shape[1] // CONV_CHUNK_COLS
    for r0 in range(0, tm, CONV_CHUNK_ROWS):
        def col_step(c, carry, r0=r0):
            cols = pl.ds(pl.multiple_of(c * CONV_CHUNK_COLS, CONV_CHUNK_COLS), CONV_CHUNK_COLS)
            a = src_ref[pl.ds(r0, ext_rows), cols]
            acc = None
            for phase in range(V7X_SUBLANES):
                taps = [k for k in range(n_taps) if (base + k) % V7X_SUBLANES == phase]
                if not taps:
                    continue
                shifted = a if phase == 0 else pltpu.roll(a, ext_rows - phase, axis=0)
                for k in taps:
                    off = (base + k) // V7X_SUBLANES * V7X_SUBLANES
                    term = shifted[off:off + CONV_CHUNK_ROWS, :] * w_ref[pl.ds(k, 1), cols]
                    acc = term if acc is None else acc + term
            emit(pl.ds(r0, CONV_CHUNK_ROWS), cols, acc)
            return carry
        lax.fori_loop(0, n_col_chunks, col_step, 0)


def _mix_kernel(av_ref, ag_ref, sb_ref, sc_ref, sx_ref, q_ref, gates_ref,
                hav_ref, hag_ref, hsc_ref, hsx_ref, x_ref, kt_ref, v_ref,
                caw_ref, cab_ref, lng_ref, lnb_ref, wa_ref, cbw_ref, wb_ref, wx_ref,
                wo_ref, gpost_ref, o_ref,
                a_ext, p_ext, conv_a, act_a, act_b, act_x, *, tm, d, n_taps_a, n_taps_b):
    not_first = pl.program_id(0) > 0
    tile = pl.ds(HALO_ROWS, tm)
    halo = pl.ds(0, HALO_ROWS)

    a_halo = hav_ref[...].astype(F32) * jax.nn.sigmoid(hag_ref[...].astype(F32))
    a_ext[halo, :] = jnp.where(not_first, a_halo, 0.0)
    p_halo = hsc_ref[...].astype(F32) * hsx_ref[...].astype(F32)
    p_ext[halo, :] = jnp.where(not_first, p_halo, 0.0)

    def fill(rows):
        dst = pl.ds(pl.multiple_of(HALO_ROWS + rows.start, NORM_CHUNK_ROWS), rows.size)
        a_ext[dst, :] = av_ref[rows, :].astype(F32) * jax.nn.sigmoid(ag_ref[rows, :].astype(F32))
        p_ext[dst, :] = sc_ref[rows, :].astype(F32) * sx_ref[rows, :].astype(F32)
    _for_row_chunks(tm, NORM_CHUNK_ROWS, fill)

    def emit_a(rows, cols, acc):
        conv_a[rows, cols] = acc + cab_ref[:, cols]
    _causal_conv_chunks(a_ext, caw_ref, n_taps_a, tm, emit_a)

    def ln_swish(rows):
        c = conv_a[rows, :]
        mu = jnp.mean(c, axis=-1, keepdims=True)
        cen = c - mu
        var = jnp.mean(cen * cen, axis=-1, keepdims=True)
        y = cen * lax.rsqrt(var + EPS) * lng_ref[...] + lnb_ref[...]
        act_a[rows, :] = (y * jax.nn.sigmoid(y)).astype(BF16)
    _for_row_chunks(tm, NORM_CHUNK_ROWS, ln_swish)

    def emit_b(rows, cols, acc):
        act_b[rows, cols] = (sb_ref[rows, cols].astype(F32) * acc).astype(BF16)
    _causal_conv_chunks(p_ext, cbw_ref, n_taps_b, tm, emit_b)

    hd = kt_ref.shape[0] // XATTN_HEADS
    for h in range(XATTN_HEADS):
        hs = pl.ds(h * hd, hd)
        s = jnp.dot(q_ref[:, hs], kt_ref[hs, :], preferred_element_type=F32)
        e = jnp.exp(s - jnp.max(s, axis=-1, keepdims=True))
        p = e * (1.0 / jnp.sum(e, axis=-1, keepdims=True))
        act_x[:, hs] = jnp.dot(p.astype(BF16), v_ref[:, hs],
                               preferred_element_type=F32).astype(BF16)

    merged = None
    for b, (act, w) in enumerate(((act_a, wa_ref), (act_b, wb_ref), (act_x, wx_ref))):
        y = jnp.dot(act[...], w[...], preferred_element_type=F32)
        gate = jax.nn.sigmoid(gates_ref[:, pl.ds(b * d, d)].astype(F32))
        merged = gate * y if merged is None else merged + gate * y
    z = jnp.dot(merged.astype(BF16), wo_ref[...], preferred_element_type=F32)
    o_ref[...] = x_ref[...] + _rms_rows(z, gpost_ref[...])


def _mix(x, proj, kt, v, caw, cab, lng, lnb, wa_bf, cbw, wb_bf, wx_bf, wo_bf, gpost, layer, *, tm):
    s, d = x.shape
    cdim = caw.shape[2]
    m, xdim = v.shape[1], v.shape[2]
    assert xdim == cdim
    n_taps_a, n_taps_b = caw.shape[1], cbw.shape[1]
    assert max(n_taps_a, n_taps_b) - 1 <= HALO_ROWS and tm % HALO_ROWS == 0
    halo_per_tile = tm // HALO_ROWS
    col = dict(av=0, ag=1, sb=2, sc=3, sx=4, q=5)
    gates_block = (6 * cdim) // (N_BRANCH * d)
    assert gates_block * N_BRANCH * d == 6 * cdim

    def tile_spec(c):
        return pl.BlockSpec((tm, cdim), lambda i: (i, c))

    def halo_spec(c):
        return pl.BlockSpec((HALO_ROWS, cdim), lambda i: (jnp.maximum(i * halo_per_tile - 1, 0), c))

    def layer_spec(shape):
        zeros = (0,) * len(shape)
        return pl.BlockSpec((None,) + shape, lambda i: (layer,) + zeros,
                            pipeline_mode=pl.Buffered(1))

    in_specs = [
        tile_spec(col["av"]), tile_spec(col["ag"]), tile_spec(col["sb"]), tile_spec(col["sc"]),
        tile_spec(col["sx"]), tile_spec(col["q"]),
        pl.BlockSpec((tm, N_BRANCH * d), lambda i: (i, gates_block)),
        halo_spec(col["av"]), halo_spec(col["ag"]), halo_spec(col["sc"]), halo_spec(col["sx"]),
        pl.BlockSpec((tm, d), lambda i: (i, 0)),
        layer_spec((xdim, m)), layer_spec((m, xdim)),
        layer_spec((n_taps_a, cdim)), layer_spec((1, cdim)), layer_spec((1, cdim)),
        layer_spec((1, cdim)), layer_spec((cdim, d)),
        layer_spec((n_taps_b, cdim)), layer_spec((cdim, d)), layer_spec((cdim, d)),
        layer_spec((d, d)), layer_spec((1, d)),
    ]
    weights = 3 * cdim * d * 2 + d * d * 2 + 2 * cdim * m * 2
    tiles = 2 * (6 * tm * cdim * 2 + tm * N_BRANCH * d * 2 + 4 * HALO_ROWS * cdim * 2 + 2 * tm * d * 4)
    scratch = 2 * (tm + HALO_ROWS) * cdim * 4 + tm * cdim * 4 + 3 * tm * cdim * 2
    temps = 6 * tm * d * 4
    kernel = functools.partial(_mix_kernel, tm=tm, d=d, n_taps_a=n_taps_a, n_taps_b=n_taps_b)
    return pl.pallas_call(
        kernel,
        grid=(s // tm,),
        in_specs=in_specs,
        out_specs=pl.BlockSpec((tm, d), lambda i: (i, 0)),
        out_shape=jax.ShapeDtypeStruct((s, d), F32),
        scratch_shapes=[
            pltpu.VMEM((tm + HALO_ROWS, cdim), F32),
            pltpu.VMEM((tm + HALO_ROWS, cdim), F32),
            pltpu.VMEM((tm, cdim), F32),
            pltpu.VMEM((tm, cdim), BF16),
            pltpu.VMEM((tm, cdim), BF16),
            pltpu.VMEM((tm, cdim), BF16),
        ],
        compiler_params=pltpu.CompilerParams(
            dimension_semantics=("arbitrary",),
            vmem_limit_bytes=_vmem_limit(weights + tiles + scratch + temps)),
        name="token_mix",
    )(proj, proj, proj, proj, proj, proj, proj, proj, proj, proj, proj, x, kt, v,
      caw, cab, lng, lnb, wa_bf, cbw, wb_bf, wx_bf, wo_bf, gpost)


def _mlp_kernel(x_ref, gpre_ref, wup_ref, wdn_ref, gpost_ref, o_ref, h_ref, inv_ref):
    j = pl.program_id(1)

    @pl.when(j == 0)
    def _():
        def emit(rows, cols, y):
            h_ref[rows, cols] = y.astype(BF16)
            o_ref[rows, cols] = jnp.zeros(y.shape, F32)
        _rms_tile(x_ref, inv_ref, gpre_ref, emit)

    up = jnp.dot(h_ref[...], wup_ref[...].astype(BF16), preferred_element_type=F32)
    r = jnp.square(jnp.maximum(up, 0.0)).astype(BF16)
    o_ref[...] += jnp.dot(r, wdn_ref[...].astype(BF16), preferred_element_type=F32)

    @pl.when(j == pl.num_programs(1) - 1)
    def _():
        def emit(rows, cols, y):
            o_ref[rows, cols] = x_ref[rows, cols] + y
        _rms_tile(o_ref, inv_ref, gpost_ref, emit)


def _mlp(x, g_pre, w_up, w_down, g_post, layer, *, tm, tf):
    s, d = x.shape
    f = w_up.shape[2]
    est = (2 * tm * d * 4 + tm * d * 2 + 4 * d * tf * 4 + 2 * d * tf * 2 + 2 * tm * d * 4
           + tm * tf * 6)
    return pl.pallas_call(
        _mlp_kernel,
        grid=(s // tm, f // tf),
        in_specs=[
            pl.BlockSpec((tm, d), lambda i, j: (i, 0)),
            pl.BlockSpec((None, 1, d), lambda i, j: (layer, 0, 0)),
            pl.BlockSpec((None, d, tf), lambda i, j: (layer, 0, j)),
            pl.BlockSpec((None, tf, d), lambda i, j: (layer, j, 0)),
            pl.BlockSpec((None, 1, d), lambda i, j: (layer, 0, 0)),
        ],
        out_specs=pl.BlockSpec((tm, d), lambda i, j: (i, 0)),
        out_shape=jax.ShapeDtypeStruct((s, d), F32),
        scratch_shapes=[pltpu.VMEM((tm, d), BF16), pltpu.VMEM((tm, V7X_LANES), F32)],
        compiler_params=pltpu.CompilerParams(
            dimension_semantics=("parallel", "arbitrary"),
            vmem_limit_bytes=_vmem_limit(est)),
        name="mlp",
    )(x, g_pre, w_up, w_down, g_post)


def kernel(x, mem, g_mix_pre, w_in, conv_a_w, conv_a_b, ln_a_g, ln_a_b, w_a_out, conv_b_w, w_b_out, g_mem, w_kv, w_x_out, w_o, g_mix_post, g_mlp_pre, w_up, w_down, g_mlp_post):
    batch, s, d = x.shape
    assert batch == 1
    depth = w_in.shape[0]

    def row(p):
        return p.reshape(p.shape[0], 1, p.shape[1])

    bf = lambda w: w.astype(BF16)
    wa_bf, wb_bf, wx_bf, wo_bf = bf(w_a_out), bf(w_b_out), bf(w_x_out), bf(w_o)
    g_mix_pre, g_mix_post, g_mlp_pre, g_mlp_post, g_mem = map(
        row, (g_mix_pre, g_mix_post, g_mlp_pre, g_mlp_post, g_mem))
    conv_a_b, ln_a_g, ln_a_b = map(row, (conv_a_b, ln_a_g, ln_a_b))

    kt, v = _memory_kv(mem[0], g_mem, w_kv)
    xs = x[0]
    for l in range(depth):
        proj = _in_proj(xs, g_mix_pre, w_in, l, tm=1024, tn=1536)
        xs = _mix(xs, proj, kt, v, conv_a_w, conv_a_b, ln_a_g, ln_a_b, wa_bf, conv_b_w,
                  wb_bf, wx_bf, wo_bf, g_mix_post, l, tm=256)
        xs = _mlp(xs, g_mlp_pre, w_up, w_down, g_mlp_post, l, tm=1024, tf=512)
    return xs[None]
```

```python
import functools

import jax
import jax.numpy as jnp
from jax import lax
from jax.experimental import pallas as pl
from jax.experimental.pallas import tpu as pltpu

F32 = jnp.float32
BF16 = jnp.bfloat16

EPS = 1e-6
XATTN_HEADS = 4
N_BRANCH = 3

V7X_VMEM_BYTES = 64 * 1024 * 1024
V7X_LANES = 128
V7X_SUBLANES = 8

HALO_ROWS = 32
NORM_CHUNK_ROWS = 32
STATS_UNROLL = 8
CONV_CHUNK_ROWS = 64
CONV_CHUNK_COLS = 128


def _vmem_limit(estimate_bytes):
    return int(min(V7X_VMEM_BYTES - 4 * 1024 * 1024, estimate_bytes * 5 // 4))


def _rms_rows(x, g):
    ms = jnp.mean(x * x, axis=-1, keepdims=True)
    return x * lax.rsqrt(ms + EPS) * g


def _for_row_chunks(n_rows, chunk, body, unroll=2):
    def step(c, carry):
        body(pl.ds(pl.multiple_of(c * chunk, chunk), chunk))
        return carry
    lax.fori_loop(0, n_rows // chunk, step, 0, unroll=unroll)


def _rms_tile(src_ref, inv_ref, g_ref, emit):
    n_rows, width = src_ref.shape

    def stats(rows):
        v = src_ref[rows, :]
        inv = lax.rsqrt(jnp.mean(v * v, axis=-1, keepdims=True) + EPS)
        inv_ref[rows, :] = jnp.broadcast_to(inv, (rows.size, V7X_LANES))
    _for_row_chunks(n_rows, NORM_CHUNK_ROWS, stats, unroll=STATS_UNROLL)

    def apply(rows):
        inv = inv_ref[rows, :]
        for c0 in range(0, width, V7X_LANES):
            cols = pl.ds(c0, V7X_LANES)
            emit(rows, cols, src_ref[rows, cols] * inv * g_ref[:, cols])
    _for_row_chunks(n_rows, NORM_CHUNK_ROWS, apply)


def _causal_conv_chunk(src_ref, w_ref, n_taps, r0, cols, emit):
    base = HALO_ROWS - (n_taps - 1)
    ext_rows = CONV_CHUNK_ROWS + HALO_ROWS
    a = src_ref[pl.ds(r0, ext_rows), cols]
    acc = None
    for phase in range(V7X_SUBLANES):
        taps = [k for k in range(n_taps) if (base + k) % V7X_SUBLANES == phase]
        if not taps:
            continue
        shifted = a if phase == 0 else pltpu.roll(a, ext_rows - phase, axis=0)
        for k in taps:
            off = (base + k) // V7X_SUBLANES * V7X_SUBLANES
            term = shifted[off:off + CONV_CHUNK_ROWS, :] * w_ref[pl.ds(k, 1), cols]
            acc = term if acc is None else acc + term
    emit(pl.ds(r0, CONV_CHUNK_ROWS), cols, acc)


def _causal_conv_tile(src_ref, w_ref, n_taps, tm, emit):
    n_col_chunks = src_ref.shape[1] // CONV_CHUNK_COLS
    for r0 in range(0, tm, CONV_CHUNK_ROWS):
        def col_step(c, carry, r0=r0):
            cols = pl.ds(pl.multiple_of(c * CONV_CHUNK_COLS, CONV_CHUNK_COLS), CONV_CHUNK_COLS)
            _causal_conv_chunk(src_ref, w_ref, n_taps, r0, cols, emit)
            return carry
        lax.fori_loop(0, n_col_chunks, col_step, 0)


def _kv_kernel(mem_ref, g_ref, w_ref, kt_ref, v_ref, *, xdim, scale):
    mem_n = _rms_rows(mem_ref[...], g_ref[...]).astype(BF16)
    kv = jnp.dot(mem_n, w_ref[...].astype(BF16), preferred_element_type=F32)
    kt_ref[...] = (kv[:, :xdim] * scale).T.astype(BF16)
    v_ref[...] = kv[:, xdim:].astype(BF16)


def _memory_kv(mem, g_mem, w_kv):
    depth, d, two_x = w_kv.shape
    xdim = two_x // 2
    m = mem.shape[0]
    head_dim = xdim // XATTN_HEADS
    est = (2 * (m * d * 4 + d * two_x * 4 + 2 * m * xdim * 2) + d * two_x * 2
           + 3 * m * two_x * 4)
    return pl.pallas_call(
        functools.partial(_kv_kernel, xdim=xdim, scale=head_dim ** -0.5),
        grid=(depth,),
        in_specs=[
            pl.BlockSpec((m, d), lambda l: (0, 0)),
            pl.BlockSpec((None, 1, d), lambda l: (l, 0, 0)),
            pl.BlockSpec((None, d, two_x), lambda l: (l, 0, 0)),
        ],
        out_specs=[
            pl.BlockSpec((None, xdim, m), lambda l: (l, 0, 0)),
            pl.BlockSpec((None, m, xdim), lambda l: (l, 0, 0)),
        ],
        out_shape=[
            jax.ShapeDtypeStruct((depth, xdim, m), BF16),
            jax.ShapeDtypeStruct((depth, m, xdim), BF16),
        ],
        compiler_params=pltpu.CompilerParams(
            dimension_semantics=("arbitrary",), vmem_limit_bytes=_vmem_limit(est)),
        name="memory_kv",
    )(mem, g_mem, w_kv)


def _in_proj_kernel(x_ref, g_ref, wv_ref, wg_ref, w1_ref, w2_ref, cw_ref, cb_ref,
                    o1_ref, o2_ref, conv_ref, h_ref, inv_ref, a_ext, tail_ref, *, n_taps):
    i, j = pl.program_id(0), pl.program_id(1)
    tm, aw = x_ref.shape[0], wv_ref.shape[1]
    halo = pl.ds(0, HALO_ROWS)

    @pl.when(j == 0)
    def _():
        def emit(rows, cols, y):
            h_ref[rows, cols] = y.astype(BF16)
        _rms_tile(x_ref, inv_ref, g_ref, emit)

    @pl.when(i == 0)
    def _():
        a_ext[halo, :] = jnp.zeros((HALO_ROWS, aw), F32)

    @pl.when(i > 0)
    def _():
        a_ext[halo, :] = tail_ref[j]

    w_ag = jnp.concatenate([wv_ref[...], wg_ref[...]], axis=1).astype(BF16)
    ag = jnp.dot(h_ref[...], w_ag, preferred_element_type=F32)
    a = ag[:, :aw] * jax.nn.sigmoid(ag[:, aw:])
    a_ext[pl.ds(HALO_ROWS, tm), :] = a
    tail_ref[j] = a[tm - HALO_ROWS:, :]

    def emit_conv(rows, cols, acc):
        conv_ref[rows, cols] = acc + cb_ref[:, cols]

    for r0 in range(0, tm, CONV_CHUNK_ROWS):
        _causal_conv_chunk(a_ext, cw_ref, n_taps, r0, pl.ds(0, aw), emit_conv)

    for o_ref, w_ref in ((o1_ref, w1_ref), (o2_ref, w2_ref)):
        o_ref[...] = jnp.dot(h_ref[...], w_ref[...].astype(BF16),
                             preferred_element_type=F32).astype(o_ref.dtype)


def _in_proj(x, g_pre, w_in, conv_w, conv_b, layer, *, tm, n_col_steps):
    s, d = x.shape
    n = w_in.shape[2]
    n_taps, cdim = conv_w.shape[1], conv_w.shape[2]
    aw = cdim // n_col_steps
    r1w = cdim
    r2w = (n - 2 * cdim - n_col_steps * r1w) // n_col_steps
    assert aw == CONV_CHUNK_COLS and aw * n_col_steps == cdim
    assert r2w % V7X_LANES == 0 and 2 * cdim + n_col_steps * (r1w + r2w) == n
    assert (2 * cdim) % r1w == 0 and (2 * cdim + n_col_steps * r1w) % r2w == 0
    r1_first = (2 * cdim) // r1w
    r2_first = (2 * cdim + n_col_steps * r1w) // r2w
    gate_first = cdim // aw
    est = (2 * tm * d * 4 + tm * d * 2 + tm * V7X_LANES * 4
           + 2 * d * (2 * aw + r1w + r2w) * 4 + d * (2 * aw + r1w + r2w) * 2
           + 2 * tm * (r1w + r2w) * 2 + 2 * tm * aw * 4
           + (tm + HALO_ROWS) * aw * 4 + n_col_steps * HALO_ROWS * aw * 4 + 2 * tm * aw * 4)
    return pl.pallas_call(
        functools.partial(_in_proj_kernel, n_taps=n_taps),
        grid=(s // tm, n_col_steps),
        in_specs=[
            pl.BlockSpec((tm, d), lambda i, j: (i, 0)),
            pl.BlockSpec((None, 1, d), lambda i, j: (layer, 0, 0)),
            pl.BlockSpec((None, d, aw), lambda i, j: (layer, 0, j)),
            pl.BlockSpec((None, d, aw), lambda i, j: (layer, 0, gate_first + j)),
            pl.BlockSpec((None, d, r1w), lambda i, j: (layer, 0, r1_first + j)),
            pl.BlockSpec((None, d, r2w), lambda i, j: (layer, 0, r2_first + j)),
            pl.BlockSpec((None, n_taps, aw), lambda i, j: (layer, 0, j)),
            pl.BlockSpec((None, 1, aw), lambda i, j: (layer, 0, j)),
        ],
        out_specs=[
            pl.BlockSpec((tm, r1w), lambda i, j: (i, j)),
            pl.BlockSpec((tm, r2w), lambda i, j: (i, j)),
            pl.BlockSpec((tm, aw), lambda i, j: (i, j)),
        ],
        out_shape=[
            jax.ShapeDtypeStruct((s, n_col_steps * r1w), BF16),
            jax.ShapeDtypeStruct((s, n_col_steps * r2w), BF16),
            jax.ShapeDtypeStruct((s, cdim), F32),
        ],
        scratch_shapes=[
            pltpu.VMEM((tm, d), BF16),
            pltpu.VMEM((tm, V7X_LANES), F32),
            pltpu.VMEM((tm + HALO_ROWS, aw), F32),
            pltpu.VMEM((n_col_steps, HALO_ROWS, aw), F32),
        ],
        compiler_params=pltpu.CompilerParams(
            dimension_semantics=("arbitrary", "arbitrary"),
            vmem_limit_bytes=_vmem_limit(est)),
        name="in_proj",
    )(x, g_pre, w_in, w_in, w_in, w_in, conv_w, conv_b)


def _mix_kernel(ca_ref, sb_ref, sc_ref, sx_ref, q_ref, g0_ref, g1_ref, g2_ref,
                hsc_ref, hsx_ref, x_ref, kt_ref, v_ref,
                lng_ref, lnb_ref, wa_ref, cbw_ref, wb_ref, wx_ref, wo_ref, gpost_ref, o_ref,
                p_ext, act_a, act_b, act_x, *, tm, n_taps_b):
    not_first = pl.program_id(0) > 0

    def ln_swish(rows):
        c = ca_ref[rows, :]
        mu = jnp.mean(c, axis=-1, keepdims=True)
        cen = c - mu
        var = jnp.mean(cen * cen, axis=-1, keepdims=True)
        y = cen * lax.rsqrt(var + EPS) * lng_ref[...] + lnb_ref[...]
        act_a[rows, :] = (y * jax.nn.sigmoid(y)).astype(BF16)
    _for_row_chunks(tm, NORM_CHUNK_ROWS, ln_swish)

    p_halo = hsc_ref[...].astype(F32) * hsx_ref[...].astype(F32)
    p_ext[pl.ds(0, HALO_ROWS), :] = jnp.where(not_first, p_halo, 0.0)

    def fill(rows):
        dst = pl.ds(pl.multiple_of(HALO_ROWS + rows.start, NORM_CHUNK_ROWS), rows.size)
        p_ext[dst, :] = sc_ref[rows, :].astype(F32) * sx_ref[rows, :].astype(F32)
    _for_row_chunks(tm, NORM_CHUNK_ROWS, fill)

    def emit_b(rows, cols, acc):
        act_b[rows, cols] = (sb_ref[rows, cols].astype(F32) * acc).astype(BF16)
    _causal_conv_tile(p_ext, cbw_ref, n_taps_b, tm, emit_b)

    hd = kt_ref.shape[0] // XATTN_HEADS
    for h in range(XATTN_HEADS):
        hs = pl.ds(h * hd, hd)
        s = jnp.dot(q_ref[:, hs], kt_ref[hs, :], preferred_element_type=F32)
        e = jnp.exp(s - jnp.max(s, axis=-1, keepdims=True))
        p = e * (1.0 / jnp.sum(e, axis=-1, keepdims=True))
        act_x[:, hs] = jnp.dot(p.astype(BF16), v_ref[:, hs],
                               preferred_element_type=F32).astype(BF16)

    merged = None
    for act, w, g_ref in ((act_a, wa_ref, g0_ref), (act_b, wb_ref, g1_ref), (act_x, wx_ref, g2_ref)):
        y = jnp.dot(act[...], w[...], preferred_element_type=F32)
        gate = jax.nn.sigmoid(g_ref[...].astype(F32))
        merged = gate * y if merged is None else merged + gate * y
    z = jnp.dot(merged.astype(BF16), wo_ref[...], preferred_element_type=F32)
    o_ref[...] = x_ref[...] + _rms_rows(z, gpost_ref[...])


def _mix(x, rest1, rest2, conv_a, kt, v, lng, lnb, wa_bf, cbw, wb_bf, wx_bf, wo_bf, gpost,
         layer, *, tm):
    s, d = x.shape
    cdim = conv_a.shape[1]
    m, xdim = v.shape[1], v.shape[2]
    n_taps_b = cbw.shape[1]
    assert xdim == cdim and d == 2 * cdim
    assert rest1.shape[1] == 4 * cdim + 2 * d and rest2.shape[1] == d
    assert n_taps_b - 1 <= HALO_ROWS and tm % HALO_ROWS == 0
    halo_per_tile = tm // HALO_ROWS
    col = dict(sb=0, sc=1, sx=2, q=3)
    gate0_block = (4 * cdim) // d

    def tile_spec(c):
        return pl.BlockSpec((tm, cdim), lambda i: (i, c))

    def halo_spec(c):
        return pl.BlockSpec((HALO_ROWS, cdim), lambda i: (jnp.maximum(i * halo_per_tile - 1, 0), c))

    def layer_spec(shape):
        zeros = (0,) * len(shape)
        return pl.BlockSpec((None,) + shape, lambda i: (layer,) + zeros,
                            pipeline_mode=pl.Buffered(1))

    in_specs = [
        tile_spec(0),
        tile_spec(col["sb"]), tile_spec(col["sc"]), tile_spec(col["sx"]), tile_spec(col["q"]),
        pl.BlockSpec((tm, d), lambda i: (i, gate0_block)),
        pl.BlockSpec((tm, d), lambda i: (i, gate0_block + 1)),
        pl.BlockSpec((tm, d), lambda i: (i, 0)),
        halo_spec(col["sc"]), halo_spec(col["sx"]),
        pl.BlockSpec((tm, d), lambda i: (i, 0)),
        layer_spec((xdim, m)), layer_spec((m, xdim)),
        layer_spec((1, cdim)), layer_spec((1, cdim)), layer_spec((cdim, d)),
        layer_spec((n_taps_b, cdim)), layer_spec((cdim, d)), layer_spec((cdim, d)),
        layer_spec((d, d)), layer_spec((1, d)),
    ]
    weights = 3 * cdim * d * 2 + d * d * 2 + 2 * cdim * m * 2
    tiles = 2 * (tm * cdim * 4 + 4 * tm * cdim * 2 + N_BRANCH * tm * d * 2
                 + 2 * HALO_ROWS * cdim * 2 + 2 * tm * d * 4)
    scratch = (tm + HALO_ROWS) * cdim * 4 + 3 * tm * cdim * 2
    temps = 6 * tm * d * 4
    kernel = functools.partial(_mix_kernel, tm=tm, n_taps_b=n_taps_b)
    return pl.pallas_call(
        kernel,
        grid=(s // tm,),
        in_specs=in_specs,
        out_specs=pl.BlockSpec((tm, d), lambda i: (i, 0)),
        out_shape=jax.ShapeDtypeStruct((s, d), F32),
        scratch_shapes=[
            pltpu.VMEM((tm + HALO_ROWS, cdim), F32),
            pltpu.VMEM((tm, cdim), BF16),
            pltpu.VMEM((tm, cdim), BF16),
            pltpu.VMEM((tm, cdim), BF16),
        ],
        compiler_params=pltpu.CompilerParams(
            dimension_semantics=("arbitrary",),
            vmem_limit_bytes=_vmem_limit(weights + tiles + scratch + temps)),
        name="token_mix",
    )(conv_a, rest1, rest1, rest1, rest1, rest1, rest1, rest2, rest1, rest1, x, kt, v,
      lng, lnb, wa_bf, cbw, wb_bf, wx_bf, wo_bf, gpost)


def _mlp_kernel(x_ref, gpre_ref, wup_ref, wdn_ref, gpost_ref, o_ref, h_ref, inv_ref):
    j = pl.program_id(1)

    @pl.when(j == 0)
    def _():
        def emit(rows, cols, y):
            h_ref[rows, cols] = y.astype(BF16)
            o_ref[rows, cols] = jnp.zeros(y.shape, F32)
        _rms_tile(x_ref, inv_ref, gpre_ref, emit)

    up = jnp.dot(h_ref[...], wup_ref[...].astype(BF16), preferred_element_type=F32)
    r = jnp.square(jnp.maximum(up, 0.0)).astype(BF16)
    o_ref[...] += jnp.dot(r, wdn_ref[...].astype(BF16), preferred_element_type=F32)

    @pl.when(j == pl.num_programs(1) - 1)
    def _():
        def emit(rows, cols, y):
            o_ref[rows, cols] = x_ref[rows, cols] + y
        _rms_tile(o_ref, inv_ref, gpost_ref, emit)


def _mlp(x, g_pre, w_up, w_down, g_post, layer, *, tm, tf):
    s, d = x.shape
    f = w_up.shape[2]
    est = (2 * tm * d * 4 + tm * d * 2 + 4 * d * tf * 4 + 2 * d * tf * 2 + 2 * tm * d * 4
           + tm * tf * 6)
    return pl.pallas_call(
        _mlp_kernel,
        grid=(s // tm, f // tf),
        in_specs=[
            pl.BlockSpec((tm, d), lambda i, j: (i, 0)),
            pl.BlockSpec((None, 1, d), lambda i, j: (layer, 0, 0)),
            pl.BlockSpec((None, d, tf), lambda i, j: (layer, 0, j)),
            pl.BlockSpec((None, tf, d), lambda i, j: (layer, j, 0)),
            pl.BlockSpec((None, 1, d), lambda i, j: (layer, 0, 0)),
        ],
        out_specs=pl.BlockSpec((tm, d), lambda i, j: (i, 0)),
        out_shape=jax.ShapeDtypeStruct((s, d), F32),
        scratch_shapes=[pltpu.VMEM((tm, d), BF16), pltpu.VMEM((tm, V7X_LANES), F32)],
        compiler_params=pltpu.CompilerParams(
            dimension_semantics=("parallel", "arbitrary"),
            vmem_limit_bytes=_vmem_limit(est)),
        name="mlp",
    )(x, g_pre, w_up, w_down, g_post)


def kernel(x, mem, g_mix_pre, w_in, conv_a_w, conv_a_b, ln_a_g, ln_a_b, w_a_out, conv_b_w, w_b_out, g_mem, w_kv, w_x_out, w_o, g_mix_post, g_mlp_pre, w_up, w_down, g_mlp_post):
    batch, s, d = x.shape
    assert batch == 1
    depth = w_in.shape[0]

    def row(p):
        return p.reshape(p.shape[0], 1, p.shape[1])

    bf = lambda w: w.astype(BF16)
    wa_bf, wb_bf, wx_bf, wo_bf = bf(w_a_out), bf(w_b_out), bf(w_x_out), bf(w_o)
    g_mix_pre, g_mix_post, g_mlp_pre, g_mlp_post, g_mem = map(
        row, (g_mix_pre, g_mix_post, g_mlp_pre, g_mlp_post, g_mem))
    conv_a_b, ln_a_g, ln_a_b = map(row, (conv_a_b, ln_a_g, ln_a_b))

    kt, v = _memory_kv(mem[0], g_mem, w_kv)
    xs = x[0]
    for l in range(depth):
        rest1, rest2, conv_a = _in_proj(xs, g_mix_pre, w_in, conv_a_w, conv_a_b, l,
                                        tm=1024, n_col_steps=8)
        xs = _mix(xs, rest1, rest2, conv_a, kt, v, ln_a_g, ln_a_b, wa_bf, conv_b_w,
                  wb_bf, wx_bf, wo_bf, g_mix_post, l, tm=256)
        xs = _mlp(xs, g_mlp_pre, w_up, w_down, g_mlp_post, l, tm=1024, tf=512)
    return xs[None]
```

```python
import functools

import jax
import jax.numpy as jnp
from jax import lax
from jax.experimental import pallas as pl
from jax.experimental.pallas import tpu as pltpu

F32 = jnp.float32
BF16 = jnp.bfloat16

EPS = 1e-6
XATTN_HEADS = 4
N_BRANCH = 3

V7X_VMEM_BYTES = 64 * 1024 * 1024
V7X_LANES = 128
V7X_SUBLANES = 8

HALO_ROWS = 32
NORM_CHUNK_ROWS = 32
STATS_UNROLL = 8
CONV_CHUNK_ROWS = 64
CHAINED_CONV_ROWS = 16
CONV_CHUNK_COLS = 128


def _vmem_limit(estimate_bytes):
    return int(min(V7X_VMEM_BYTES - 4 * 1024 * 1024, estimate_bytes * 5 // 4))


def _rms_rows(x, g):
    ms = jnp.mean(x * x, axis=-1, keepdims=True)
    return x * lax.rsqrt(ms + EPS) * g


def _for_row_chunks(n_rows, chunk, body, unroll=2):
    def step(c, carry):
        body(pl.ds(pl.multiple_of(c * chunk, chunk), chunk))
        return carry
    lax.fori_loop(0, n_rows // chunk, step, 0, unroll=unroll)


def _rms_tile(src_ref, inv_ref, g_ref, emit):
    n_rows, width = src_ref.shape

    def stats(rows):
        v = src_ref[rows, :]
        inv = lax.rsqrt(jnp.mean(v * v, axis=-1, keepdims=True) + EPS)
        inv_ref[rows, :] = jnp.broadcast_to(inv, (rows.size, V7X_LANES))
    _for_row_chunks(n_rows, NORM_CHUNK_ROWS, stats, unroll=STATS_UNROLL)

    def apply(rows):
        inv = inv_ref[rows, :]
        for c0 in range(0, width, V7X_LANES):
            cols = pl.ds(c0, V7X_LANES)
            emit(rows, cols, src_ref[rows, cols] * inv * g_ref[:, cols])
    _for_row_chunks(n_rows, NORM_CHUNK_ROWS, apply)


def _zero_after(v):
    bits = lax.bitcast_convert_type(v, jnp.uint32)
    cleared = lax.shift_right_logical(lax.shift_right_logical(bits, jnp.uint32(16)),
                                      jnp.uint32(16))
    return lax.bitcast_convert_type(cleared, F32)


def _causal_conv_chunk(src_ref, w_ref, n_taps, r0, cols, emit, n_rows=None, after=None):
    n_rows = n_rows or CONV_CHUNK_ROWS
    base = HALO_ROWS - (n_taps - 1)
    ext_rows = n_rows + HALO_ROWS
    a = src_ref[pl.ds(r0, ext_rows), cols]
    if after is not None:
        a = a + jnp.concatenate([after] * (ext_rows // V7X_SUBLANES), axis=0)
    acc = None
    for phase in range(V7X_SUBLANES):
        taps = [k for k in range(n_taps) if (base + k) % V7X_SUBLANES == phase]
        if not taps:
            continue
        shifted = a if phase == 0 else pltpu.roll(a, ext_rows - phase, axis=0)
        for k in taps:
            off = (base + k) // V7X_SUBLANES * V7X_SUBLANES
            term = shifted[off:off + n_rows, :] * w_ref[pl.ds(k, 1), cols]
            acc = term if acc is None else acc + term
    emit(pl.ds(r0, n_rows), cols, acc)
    return _zero_after(acc[:V7X_SUBLANES, :])


def _kv_kernel(mem_ref, g_ref, w_ref, kt_ref, v_ref, *, xdim, scale):
    mem_n = _rms_rows(mem_ref[...], g_ref[...]).astype(BF16)
    kv = jnp.dot(mem_n, w_ref[...].astype(BF16), preferred_element_type=F32)
    kt_ref[...] = (kv[:, :xdim] * scale).T.astype(BF16)
    v_ref[...] = kv[:, xdim:].astype(BF16)


def _memory_kv(mem, g_mem, w_kv):
    depth, d, two_x = w_kv.shape
    xdim = two_x // 2
    m = mem.shape[0]
    head_dim = xdim // XATTN_HEADS
    est = (2 * (m * d * 4 + d * two_x * 4 + 2 * m * xdim * 2) + d * two_x * 2
           + 3 * m * two_x * 4)
    return pl.pallas_call(
        functools.partial(_kv_kernel, xdim=xdim, scale=head_dim ** -0.5),
        grid=(depth,),
        in_specs=[
            pl.BlockSpec((m, d), lambda l: (0, 0)),
            pl.BlockSpec((None, 1, d), lambda l: (l, 0, 0)),
            pl.BlockSpec((None, d, two_x), lambda l: (l, 0, 0)),
        ],
        out_specs=[
            pl.BlockSpec((None, xdim, m), lambda l: (l, 0, 0)),
            pl.BlockSpec((None, m, xdim), lambda l: (l, 0, 0)),
        ],
        out_shape=[
            jax.ShapeDtypeStruct((depth, xdim, m), BF16),
            jax.ShapeDtypeStruct((depth, m, xdim), BF16),
        ],
        compiler_params=pltpu.CompilerParams(
            dimension_semantics=("arbitrary",), vmem_limit_bytes=_vmem_limit(est)),
        name="memory_kv",
    )(mem, g_mem, w_kv)


def _in_proj_kernel(x_ref, g_ref, wv_ref, wg_ref, w1_ref, w2_ref, cw_ref, cb_ref,
                    o1_ref, o2_ref, conv_ref, h_ref, inv_ref, a_ext, tail_ref, *, n_taps):
    i, j = pl.program_id(0), pl.program_id(1)
    tm, aw = x_ref.shape[0], wv_ref.shape[1]
    halo = pl.ds(0, HALO_ROWS)

    @pl.when(j == 0)
    def _():
        def emit(rows, cols, y):
            h_ref[rows, cols] = y.astype(BF16)
        _rms_tile(x_ref, inv_ref, g_ref, emit)

    @pl.when(i == 0)
    def _():
        a_ext[halo, :] = jnp.zeros((HALO_ROWS, aw), F32)

    @pl.when(i > 0)
    def _():
        a_ext[halo, :] = tail_ref[j]

    w_ag = jnp.concatenate([wv_ref[...], wg_ref[...]], axis=1).astype(BF16)
    ag = jnp.dot(h_ref[...], w_ag, preferred_element_type=F32)
    a = ag[:, :aw] * jax.nn.sigmoid(ag[:, aw:])
    a_ext[pl.ds(HALO_ROWS, tm), :] = a
    tail_ref[j] = a[tm - HALO_ROWS:, :]

    def emit_conv(rows, cols, acc):
        conv_ref[rows, cols] = acc + cb_ref[:, cols]

    for o_ref, w_ref in ((o1_ref, w1_ref), (o2_ref, w2_ref)):
        o_ref[...] = jnp.dot(h_ref[...], w_ref[...].astype(BF16),
                             preferred_element_type=F32).astype(o_ref.dtype)

    after = _zero_after(a[:V7X_SUBLANES, :])
    for r0 in range(0, tm, CHAINED_CONV_ROWS):
        after = _causal_conv_chunk(a_ext, cw_ref, n_taps, r0, pl.ds(0, aw), emit_conv,
                                   n_rows=CHAINED_CONV_ROWS, after=after)


def _in_proj(x, g_pre, w_in, conv_w, conv_b, layer, *, tm, n_col_steps):
    s, d = x.shape
    n = w_in.shape[2]
    n_taps, cdim = conv_w.shape[1], conv_w.shape[2]
    aw = cdim // n_col_steps
    r1w = cdim
    r2w = (n - 2 * cdim - n_col_steps * r1w) // n_col_steps
    assert aw == CONV_CHUNK_COLS and aw * n_col_steps == cdim
    assert r2w % V7X_LANES == 0 and 2 * cdim + n_col_steps * (r1w + r2w) == n
    assert (2 * cdim) % r1w == 0 and (2 * cdim + n_col_steps * r1w) % r2w == 0
    r1_first = (2 * cdim) // r1w
    r2_first = (2 * cdim + n_col_steps * r1w) // r2w
    gate_first = cdim // aw
    est = (2 * tm * d * 4 + tm * d * 2 + tm * V7X_LANES * 4
           + 2 * d * (2 * aw + r1w + r2w) * 4 + d * (2 * aw + r1w + r2w) * 2
           + 2 * tm * (r1w + r2w) * 2 + 2 * tm * aw * 4
           + (tm + HALO_ROWS) * aw * 4 + n_col_steps * HALO_ROWS * aw * 4 + 2 * tm * aw * 4)
    return pl.pallas_call(
        functools.partial(_in_proj_kernel, n_taps=n_taps),
        grid=(s // tm, n_col_steps),
        in_specs=[
            pl.BlockSpec((tm, d), lambda i, j: (i, 0)),
            pl.BlockSpec((None, 1, d), lambda i, j: (layer, 0, 0)),
            pl.BlockSpec((None, d, aw), lambda i, j: (layer, 0, j)),
            pl.BlockSpec((None, d, aw), lambda i, j: (layer, 0, gate_first + j)),
            pl.BlockSpec((None, d, r1w), lambda i, j: (layer, 0, r1_first + j)),
            pl.BlockSpec((None, d, r2w), lambda i, j: (layer, 0, r2_first + j)),
            pl.BlockSpec((None, n_taps, aw), lambda i, j: (layer, 0, j)),
            pl.BlockSpec((None, 1, aw), lambda i, j: (layer, 0, j)),
        ],
        out_specs=[
            pl.BlockSpec((tm, r1w), lambda i, j: (i, j)),
            pl.BlockSpec((tm, r2w), lambda i, j: (i, j)),
            pl.BlockSpec((tm, aw), lambda i, j: (i, j)),
        ],
        out_shape=[
            jax.ShapeDtypeStruct((s, n_col_steps * r1w), BF16),
            jax.ShapeDtypeStruct((s, n_col_steps * r2w), BF16),
            jax.ShapeDtypeStruct((s, cdim), F32),
        ],
        scratch_shapes=[
            pltpu.VMEM((tm, d), BF16),
            pltpu.VMEM((tm, V7X_LANES), F32),
            pltpu.VMEM((tm + HALO_ROWS, aw), F32),
            pltpu.VMEM((n_col_steps, HALO_ROWS, aw), F32),
        ],
        compiler_params=pltpu.CompilerParams(
            dimension_semantics=("arbitrary", "arbitrary"),
            vmem_limit_bytes=_vmem_limit(est)),
        name="in_proj",
    )(x, g_pre, w_in, w_in, w_in, w_in, conv_w, conv_b)


def _mix_kernel(ca_ref, sb_ref, sc_ref, sx_ref, q_ref, g0_ref, g1_ref, g2_ref,
                hsc_ref, hsx_ref, x_ref, kt_ref, v_ref,
                lng_ref, lnb_ref, wa_ref, cbw_ref, wb_ref, wx_ref, wo_ref, gpost_ref, o_ref,
                p_ext, act_a, act_b, act_x, *, tm, n_taps_b):
    not_first = pl.program_id(0) > 0
    cdim = ca_ref.shape[1]

    hd = kt_ref.shape[0] // XATTN_HEADS
    for h in range(XATTN_HEADS):
        hs = pl.ds(h * hd, hd)
        s = jnp.dot(q_ref[:, hs], kt_ref[hs, :], preferred_element_type=F32)
        e = jnp.exp(s - jnp.max(s, axis=-1, keepdims=True))
        p = e * (1.0 / jnp.sum(e, axis=-1, keepdims=True))
        act_x[:, hs] = jnp.dot(p.astype(BF16), v_ref[:, hs],
                               preferred_element_type=F32).astype(BF16)
    y_x = jnp.dot(act_x[...], wx_ref[...], preferred_element_type=F32)

    p_halo = hsc_ref[...].astype(F32) * hsx_ref[...].astype(F32)
    p_ext[pl.ds(0, HALO_ROWS), :] = jnp.where(not_first, p_halo, 0.0)
    for r0 in range(0, tm, NORM_CHUNK_ROWS):
        rows = pl.ds(r0, NORM_CHUNK_ROWS)
        p_ext[pl.ds(HALO_ROWS + r0, NORM_CHUNK_ROWS), :] = (
            sc_ref[rows, :].astype(F32) * sx_ref[rows, :].astype(F32))

    def emit_b(rows, cols, acc):
        act_b[rows, cols] = (sb_ref[rows, cols].astype(F32) * acc).astype(BF16)
    for r0 in range(0, tm, CONV_CHUNK_ROWS):
        for c0 in range(0, cdim, CONV_CHUNK_COLS):
            _causal_conv_chunk(p_ext, cbw_ref, n_taps_b, r0, pl.ds(c0, CONV_CHUNK_COLS), emit_b)
    y_b = jnp.dot(act_b[...], wb_ref[...], preferred_element_type=F32)

    for r0 in range(0, tm, NORM_CHUNK_ROWS):
        rows = pl.ds(r0, NORM_CHUNK_ROWS)
        c = ca_ref[rows, :]
        mu = jnp.mean(c, axis=-1, keepdims=True)
        cen = c - mu
        var = jnp.mean(cen * cen, axis=-1, keepdims=True)
        y = cen * lax.rsqrt(var + EPS) * lng_ref[...] + lnb_ref[...]
        act_a[rows, :] = (y * jax.nn.sigmoid(y)).astype(BF16)
    y_a = jnp.dot(act_a[...], wa_ref[...], preferred_element_type=F32)

    merged = None
    for y, g_ref in ((y_a, g0_ref), (y_b, g1_ref), (y_x, g2_ref)):
        gate = jax.nn.sigmoid(g_ref[...].astype(F32))
        merged = gate * y if merged is None else merged + gate * y
    z = jnp.dot(merged.astype(BF16), wo_ref[...], preferred_element_type=F32)
    o_ref[...] = x_ref[...] + _rms_rows(z, gpost_ref[...])


def _mix(x, rest1, rest2, conv_a, kt, v, lng, lnb, wa_bf, cbw, wb_bf, wx_bf, wo_bf, gpost,
         layer, *, tm):
    s, d = x.shape
    cdim = conv_a.shape[1]
    m, xdim = v.shape[1], v.shape[2]
    n_taps_b = cbw.shape[1]
    assert xdim == cdim and d == 2 * cdim
    assert rest1.shape[1] == 4 * cdim + 2 * d and rest2.shape[1] == d
    assert n_taps_b - 1 <= HALO_ROWS and tm % HALO_ROWS == 0
    halo_per_tile = tm // HALO_ROWS
    col = dict(sb=0, sc=1, sx=2, q=3)
    gate0_block = (4 * cdim) // d

    def tile_spec(c):
        return pl.BlockSpec((tm, cdim), lambda i: (i, c))

    def halo_spec(c):
        return pl.BlockSpec((HALO_ROWS, cdim), lambda i: (jnp.maximum(i * halo_per_tile - 1, 0), c))

    def layer_spec(shape):
        zeros = (0,) * len(shape)
        return pl.BlockSpec((None,) + shape, lambda i: (layer,) + zeros,
                            pipeline_mode=pl.Buffered(1))

    in_specs = [
        tile_spec(0),
        tile_spec(col["sb"]), tile_spec(col["sc"]), tile_spec(col["sx"]), tile_spec(col["q"]),
        pl.BlockSpec((tm, d), lambda i: (i, gate0_block)),
        pl.BlockSpec((tm, d), lambda i: (i, gate0_block + 1)),
        pl.BlockSpec((tm, d), lambda i: (i, 0)),
        halo_spec(col["sc"]), halo_spec(col["sx"]),
        pl.BlockSpec((tm, d), lambda i: (i, 0)),
        layer_spec((xdim, m)), layer_spec((m, xdim)),
        layer_spec((1, cdim)), layer_spec((1, cdim)), layer_spec((cdim, d)),
        layer_spec((n_taps_b, cdim)), layer_spec((cdim, d)), layer_spec((cdim, d)),
        layer_spec((d, d)), layer_spec((1, d)),
    ]
    weights = 3 * cdim * d * 2 + d * d * 2 + 2 * cdim * m * 2
    tiles = 2 * (tm * cdim * 4 + 4 * tm * cdim * 2 + N_BRANCH * tm * d * 2
                 + 2 * HALO_ROWS * cdim * 2 + 2 * tm * d * 4)
    scratch = (tm + HALO_ROWS) * cdim * 4 + 3 * tm * cdim * 2
    temps = 6 * tm * d * 4
    kernel = functools.partial(_mix_kernel, tm=tm, n_taps_b=n_taps_b)
    return pl.pallas_call(
        kernel,
        grid=(s // tm,),
        in_specs=in_specs,
        out_specs=pl.BlockSpec((tm, d), lambda i: (i, 0)),
        out_shape=jax.ShapeDtypeStruct((s, d), F32),
        scratch_shapes=[
            pltpu.VMEM((tm + HALO_ROWS, cdim), F32),
            pltpu.VMEM((tm, cdim), BF16),
            pltpu.VMEM((tm, cdim), BF16),
            pltpu.VMEM((tm, cdim), BF16),
        ],
        compiler_params=pltpu.CompilerParams(
            dimension_semantics=("arbitrary",),
            vmem_limit_bytes=_vmem_limit(weights + tiles + scratch + temps)),
        name="token_mix",
    )(conv_a, rest1, rest1, rest1, rest1, rest1, rest1, rest2, rest1, rest1, x, kt, v,
      lng, lnb, wa_bf, cbw, wb_bf, wx_bf, wo_bf, gpost)


def _mlp_kernel(x_ref, gpre_ref, wup_ref, wdn_ref, gpost_ref, o_ref, h_ref, inv_ref):
    j = pl.program_id(1)

    @pl.when(j == 0)
    def _():
        def emit(rows, cols, y):
            h_ref[rows, cols] = y.astype(BF16)
            o_ref[rows, cols] = jnp.zeros(y.shape, F32)
        _rms_tile(x_ref, inv_ref, gpre_ref, emit)

    up = jnp.dot(h_ref[...], wup_ref[...].astype(BF16), preferred_element_type=F32)
    r = jnp.square(jnp.maximum(up, 0.0)).astype(BF16)
    o_ref[...] += jnp.dot(r, wdn_ref[...].astype(BF16), preferred_element_type=F32)

    @pl.when(j == pl.num_programs(1) - 1)
    def _():
        def emit(rows, cols, y):
            o_ref[rows, cols] = x_ref[rows, cols] + y
        _rms_tile(o_ref, inv_ref, gpost_ref, emit)


def _mlp(x, g_pre, w_up, w_down, g_post, layer, *, tm, tf):
    s, d = x.shape
    f = w_up.shape[2]
    est = (2 * tm * d * 4 + tm * d * 2 + 4 * d * tf * 4 + 2 * d * tf * 2 + 2 * tm * d * 4
           + tm * tf * 6)
    return pl.pallas_call(
        _mlp_kernel,
        grid=(s // tm, f // tf),
        in_specs=[
            pl.BlockSpec((tm, d), lambda i, j: (i, 0)),
            pl.BlockSpec((None, 1, d), lambda i, j: (layer, 0, 0)),
            pl.BlockSpec((None, d, tf), lambda i, j: (layer, 0, j)),
            pl.BlockSpec((None, tf, d), lambda i, j: (layer, j, 0)),
            pl.BlockSpec((None, 1, d), lambda i, j: (layer, 0, 0)),
        ],
        out_specs=pl.BlockSpec((tm, d), lambda i, j: (i, 0)),
        out_shape=jax.ShapeDtypeStruct((s, d), F32),
        scratch_shapes=[pltpu.VMEM((tm, d), BF16), pltpu.VMEM((tm, V7X_LANES), F32)],
        compiler_params=pltpu.CompilerParams(
            dimension_semantics=("parallel", "arbitrary"),
            vmem_limit_bytes=_vmem_limit(est)),
        name="mlp",
    )(x, g_pre, w_up, w_down, g_post)


def kernel(x, mem, g_mix_pre, w_in, conv_a_w, conv_a_b, ln_a_g, ln_a_b, w_a_out, conv_b_w, w_b_out, g_mem, w_kv, w_x_out, w_o, g_mix_post, g_mlp_pre, w_up, w_down, g_mlp_post):
    batch, s, d = x.shape
    assert batch == 1
    depth = w_in.shape[0]

    def row(p):
        return p.reshape(p.shape[0], 1, p.shape[1])

    bf = lambda w: w.astype(BF16)
    wa_bf, wb_bf, wx_bf, wo_bf = bf(w_a_out), bf(w_b_out), bf(w_x_out), bf(w_o)
    g_mix_pre, g_mix_post, g_mlp_pre, g_mlp_post, g_mem = map(
        row, (g_mix_pre, g_mix_post, g_mlp_pre, g_mlp_post, g_mem))
    conv_a_b, ln_a_g, ln_a_b = map(row, (conv_a_b, ln_a_g, ln_a_b))

    kt, v = _memory_kv(mem[0], g_mem, w_kv)
    xs = x[0]
    for l in range(depth):
        rest1, rest2, conv_a = _in_proj(xs, g_mix_pre, w_in, conv_a_w, conv_a_b, l,
                                        tm=1024, n_col_steps=8)
        xs = _mix(xs, rest1, rest2, conv_a, kt, v, ln_a_g, ln_a_b, wa_bf, conv_b_w,
                  wb_bf, wx_bf, wo_bf, g_mix_post, l, tm=256)
        xs = _mlp(xs, g_mlp_pre, w_up, w_down, g_mlp_post, l, tm=1024, tf=512)
    return xs[None]
```

```python
import functools

import jax
import jax.numpy as jnp
from jax import lax
from jax.experimental import pallas as pl
from jax.experimental.pallas import tpu as pltpu

F32 = jnp.float32
BF16 = jnp.bfloat16

EPS = 1e-6
XATTN_HEADS = 4
N_BRANCH = 3

V7X_VMEM_BYTES = 64 * 1024 * 1024
V7X_LANES = 128
V7X_SUBLANES = 8

HALO_ROWS = 32
NORM_CHUNK_ROWS = 32
STATS_UNROLL = 8
CONV_CHUNK_ROWS = 64
CHAINED_CONV_ROWS = 16
LN_CHAINS = 2
CONV_CHUNK_COLS = 128


def _vmem_limit(estimate_bytes):
    return int(min(V7X_VMEM_BYTES - 4 * 1024 * 1024, estimate_bytes * 5 // 4))


def _rms_rows(x, g):
    ms = jnp.mean(x * x, axis=-1, keepdims=True)
    return x * lax.rsqrt(ms + EPS) * g


def _sigmoid(x):
    return 0.5 * jnp.tanh(0.5 * x) + 0.5


def _for_row_chunks(n_rows, chunk, body, unroll=2):
    def step(c, carry):
        body(pl.ds(pl.multiple_of(c * chunk, chunk), chunk))
        return carry
    lax.fori_loop(0, n_rows // chunk, step, 0, unroll=unroll)


def _rms_tile(src_ref, inv_ref, g_ref, emit):
    n_rows, width = src_ref.shape

    def stats(rows):
        v = src_ref[rows, :]
        inv = lax.rsqrt(jnp.mean(v * v, axis=-1, keepdims=True) + EPS)
        inv_ref[rows, :] = jnp.broadcast_to(inv, (rows.size, V7X_LANES))
    _for_row_chunks(n_rows, NORM_CHUNK_ROWS, stats, unroll=STATS_UNROLL)

    def apply(rows):
        inv = inv_ref[rows, :]
        for c0 in range(0, width, V7X_LANES):
            cols = pl.ds(c0, V7X_LANES)
            emit(rows, cols, src_ref[rows, cols] * inv * g_ref[:, cols])
    _for_row_chunks(n_rows, NORM_CHUNK_ROWS, apply)


def _zero_after(v):
    bits = lax.bitcast_convert_type(v, jnp.uint32)
    cleared = lax.shift_right_logical(lax.shift_right_logical(bits, jnp.uint32(16)),
                                      jnp.uint32(16))
    return lax.bitcast_convert_type(cleared, F32)


def _causal_conv_chunk(src_ref, w_ref, n_taps, r0, cols, emit, n_rows=None, after=None):
    n_rows = n_rows or CONV_CHUNK_ROWS
    base = HALO_ROWS - (n_taps - 1)
    ext_rows = n_rows + HALO_ROWS
    a = src_ref[pl.ds(r0, ext_rows), cols]
    if after is not None:
        a = a + jnp.concatenate([after] * (ext_rows // V7X_SUBLANES), axis=0)
    acc = None
    for phase in range(V7X_SUBLANES):
        taps = [k for k in range(n_taps) if (base + k) % V7X_SUBLANES == phase]
        if not taps:
            continue
        shifted = a if phase == 0 else pltpu.roll(a, ext_rows - phase, axis=0)
        for k in taps:
            off = (base + k) // V7X_SUBLANES * V7X_SUBLANES
            term = shifted[off:off + n_rows, :] * w_ref[pl.ds(k, 1), cols]
            acc = term if acc is None else acc + term
    emit(pl.ds(r0, n_rows), cols, acc)
    return _zero_after(acc[:V7X_SUBLANES, :])


def _kv_kernel(mem_ref, g_ref, w_ref, kt_ref, v_ref, *, xdim, scale):
    mem_n = _rms_rows(mem_ref[...], g_ref[...]).astype(BF16)
    kv = jnp.dot(mem_n, w_ref[...].astype(BF16), preferred_element_type=F32)
    kt_ref[...] = (kv[:, :xdim] * scale).T.astype(BF16)
    v_ref[...] = kv[:, xdim:].astype(BF16)


def _memory_kv(mem, g_mem, w_kv):
    depth, d, two_x = w_kv.shape
    xdim = two_x // 2
    m = mem.shape[0]
    head_dim = xdim // XATTN_HEADS
    est = (2 * (m * d * 4 + d * two_x * 4 + 2 * m * xdim * 2) + d * two_x * 2
           + 3 * m * two_x * 4)
    return pl.pallas_call(
        functools.partial(_kv_kernel, xdim=xdim, scale=head_dim ** -0.5),
        grid=(depth,),
        in_specs=[
            pl.BlockSpec((m, d), lambda l: (0, 0)),
            pl.BlockSpec((None, 1, d), lambda l: (l, 0, 0)),
            pl.BlockSpec((None, d, two_x), lambda l: (l, 0, 0)),
        ],
        out_specs=[
            pl.BlockSpec((None, xdim, m), lambda l: (l, 0, 0)),
            pl.BlockSpec((None, m, xdim), lambda l: (l, 0, 0)),
        ],
        out_shape=[
            jax.ShapeDtypeStruct((depth, xdim, m), BF16),
            jax.ShapeDtypeStruct((depth, m, xdim), BF16),
        ],
        compiler_params=pltpu.CompilerParams(
            dimension_semantics=("arbitrary",), vmem_limit_bytes=_vmem_limit(est)),
        name="memory_kv",
    )(mem, g_mem, w_kv)


def _in_proj_kernel(x_ref, g_ref, wv_ref, wg_ref, wp_ref, wgt_ref, cw_ref, cb_ref,
                    plain_ref, gates_ref, conv_ref, h_ref, inv_ref, a_ext, tail_ref, *, n_taps):
    i, j = pl.program_id(0), pl.program_id(1)
    tm, aw = x_ref.shape[0], wv_ref.shape[1]
    halo = pl.ds(0, HALO_ROWS)

    @pl.when(j == 0)
    def _():
        def emit(rows, cols, y):
            h_ref[rows, cols] = y.astype(BF16)
        _rms_tile(x_ref, inv_ref, g_ref, emit)

    @pl.when(i == 0)
    def _():
        a_ext[halo, :] = jnp.zeros((HALO_ROWS, aw), F32)

    @pl.when(i > 0)
    def _():
        a_ext[halo, :] = tail_ref[j]

    w_ag = jnp.concatenate([wv_ref[...], wg_ref[...]], axis=1).astype(BF16)
    ag = jnp.dot(h_ref[...], w_ag, preferred_element_type=F32)
    a = ag[:, :aw] * _sigmoid(ag[:, aw:])
    a_ext[pl.ds(HALO_ROWS, tm), :] = a
    tail_ref[j] = a[tm - HALO_ROWS:, :]

    slice_j = pl.ds(pl.multiple_of(j * aw, aw), aw)
    cw_j, cb_j = cw_ref.at[:, slice_j], cb_ref.at[:, slice_j]

    def emit_conv(rows, cols, acc):
        conv_ref[rows, cols] = acc + cb_j[:, cols]

    plain_ref[...] = jnp.dot(h_ref[...], wp_ref[...].astype(BF16),
                             preferred_element_type=F32).astype(plain_ref.dtype)
    gates_ref[...] = _sigmoid(jnp.dot(h_ref[...], wgt_ref[...].astype(BF16),
                                      preferred_element_type=F32)).astype(gates_ref.dtype)

    after = _zero_after(a[:V7X_SUBLANES, :])
    for r0 in range(0, tm, CHAINED_CONV_ROWS):
        after = _causal_conv_chunk(a_ext, cw_j, n_taps, r0, pl.ds(0, aw), emit_conv,
                                   n_rows=CHAINED_CONV_ROWS, after=after)


def _in_proj(x, g_pre, w_in, conv_w, conv_b, layer, *, tm, n_col_steps):
    s, d = x.shape
    n = w_in.shape[2]
    n_taps, cdim = conv_w.shape[1], conv_w.shape[2]
    n_gates = N_BRANCH * d
    n_plain = n - 2 * cdim - n_gates
    aw, pw, gw = cdim // n_col_steps, n_plain // n_col_steps, n_gates // n_col_steps
    assert aw == CONV_CHUNK_COLS and aw * n_col_steps == cdim
    assert pw % V7X_LANES == 0 and gw % V7X_LANES == 0
    assert pw * n_col_steps == n_plain and gw * n_col_steps == n_gates
    assert (2 * cdim) % pw == 0 and (2 * cdim + n_plain) % gw == 0
    gate_first = cdim // aw
    plain_first = (2 * cdim) // pw
    gates_first = (2 * cdim + n_plain) // gw
    cols = 2 * aw + pw + gw
    est = (2 * tm * d * 4 + tm * d * 2 + tm * V7X_LANES * 4
           + 2 * d * cols * 4 + d * cols * 2
           + 2 * tm * (pw + gw) * 2 + 2 * tm * aw * 4
           + (tm + HALO_ROWS) * aw * 4 + n_col_steps * HALO_ROWS * aw * 4 + 2 * tm * aw * 4)
    return pl.pallas_call(
        functools.partial(_in_proj_kernel, n_taps=n_taps),
        grid=(s // tm, n_col_steps),
        in_specs=[
            pl.BlockSpec((tm, d), lambda i, j: (i, 0)),
            pl.BlockSpec((None, 1, d), lambda i, j: (layer, 0, 0)),
            pl.BlockSpec((None, d, aw), lambda i, j: (layer, 0, j)),
            pl.BlockSpec((None, d, aw), lambda i, j: (layer, 0, gate_first + j)),
            pl.BlockSpec((None, d, pw), lambda i, j: (layer, 0, plain_first + j)),
            pl.BlockSpec((None, d, gw), lambda i, j: (layer, 0, gates_first + j)),
            pl.BlockSpec((None, n_taps, cdim), lambda i, j: (layer, 0, 0)),
            pl.BlockSpec((None, 1, cdim), lambda i, j: (layer, 0, 0)),
        ],
        out_specs=[
            pl.BlockSpec((tm, pw), lambda i, j: (i, j)),
            pl.BlockSpec((tm, gw), lambda i, j: (i, j)),
            pl.BlockSpec((tm, aw), lambda i, j: (i, j)),
        ],
        out_shape=[
            jax.ShapeDtypeStruct((s, n_plain), BF16),
            jax.ShapeDtypeStruct((s, n_gates), BF16),
            jax.ShapeDtypeStruct((s, cdim), F32),
        ],
        scratch_shapes=[
            pltpu.VMEM((tm, d), BF16),
            pltpu.VMEM((tm, V7X_LANES), F32),
            pltpu.VMEM((tm + HALO_ROWS, aw), F32),
            pltpu.VMEM((n_col_steps, HALO_ROWS, aw), F32),
        ],
        compiler_params=pltpu.CompilerParams(
            dimension_semantics=("arbitrary", "arbitrary"),
            vmem_limit_bytes=_vmem_limit(est)),
        name="in_proj",
    )(x, g_pre, w_in, w_in, w_in, w_in, conv_w, conv_b)


def _mix_kernel(ca_ref, plain_ref, gates_ref, halo_ref, x_ref, kt_ref, v_ref,
                lng_ref, lnb_ref, wa_ref, cbw_ref, wb_ref, wx_ref, wo_ref, gpost_ref, o_ref,
                p_ext, act_a, act_b, act_x, *, tm, n_taps_b):
    not_first = pl.program_id(0) > 0
    cdim, d = ca_ref.shape[1], x_ref.shape[1]
    sb_ref, sc_ref, sx_ref, q_ref = (plain_ref.at[:, pl.ds(c * cdim, cdim)] for c in range(4))
    hsc_ref, hsx_ref = (halo_ref.at[:, pl.ds(c * cdim, cdim)] for c in (1, 2))
    g0_ref, g1_ref, g2_ref = (gates_ref.at[:, pl.ds(b * d, d)] for b in range(N_BRANCH))

    hd = kt_ref.shape[0] // XATTN_HEADS
    for h in range(XATTN_HEADS):
        hs = pl.ds(h * hd, hd)
        s = jnp.dot(q_ref[:, hs], kt_ref[hs, :], preferred_element_type=F32)
        e = jnp.exp(s - jnp.max(s, axis=-1, keepdims=True))
        p = e * (1.0 / jnp.sum(e, axis=-1, keepdims=True))
        act_x[:, hs] = jnp.dot(p.astype(BF16), v_ref[:, hs],
                               preferred_element_type=F32).astype(BF16)
    y_x = jnp.dot(act_x[...], wx_ref[...], preferred_element_type=F32)

    p_halo = hsc_ref[...].astype(F32) * hsx_ref[...].astype(F32)
    p_ext[pl.ds(0, HALO_ROWS), :] = jnp.where(not_first, p_halo, 0.0)
    for r0 in range(0, tm, NORM_CHUNK_ROWS):
        rows = pl.ds(r0, NORM_CHUNK_ROWS)
        p_ext[pl.ds(HALO_ROWS + r0, NORM_CHUNK_ROWS), :] = (
            sc_ref[rows, :].astype(F32) * sx_ref[rows, :].astype(F32))

    def emit_b(rows, cols, acc):
        act_b[rows, cols] = (sb_ref[rows, cols].astype(F32) * acc).astype(BF16)
    after = None
    for r0 in range(0, tm, CONV_CHUNK_ROWS):
        for c0 in range(0, cdim, CONV_CHUNK_COLS):
            after = _causal_conv_chunk(p_ext, cbw_ref, n_taps_b, r0,
                                       pl.ds(c0, CONV_CHUNK_COLS), emit_b, after=after)
    y_b = jnp.dot(act_b[...], wb_ref[...], preferred_element_type=F32)

    chains = [after] * LN_CHAINS
    for n, r0 in enumerate(range(0, tm, NORM_CHUNK_ROWS)):
        rows = pl.ds(r0, NORM_CHUNK_ROWS)
        zeros = jnp.concatenate([chains[n % LN_CHAINS]] * (NORM_CHUNK_ROWS // V7X_SUBLANES), axis=0)
        c = jnp.concatenate([ca_ref[rows, :V7X_LANES] + zeros, ca_ref[rows, V7X_LANES:]], axis=1)
        mu = jnp.mean(c, axis=-1, keepdims=True)
        cen = c - mu
        var = jnp.mean(cen * cen, axis=-1, keepdims=True)
        y = cen * lax.rsqrt(var + EPS) * lng_ref[...] + lnb_ref[...]
        act_a[rows, :] = (y * _sigmoid(y)).astype(BF16)
        chains[n % LN_CHAINS] = _zero_after(y[:V7X_SUBLANES, :V7X_LANES])
    y_a = jnp.dot(act_a[...], wa_ref[...], preferred_element_type=F32)

    merged = None
    for y, g_ref in ((y_a, g0_ref), (y_b, g1_ref), (y_x, g2_ref)):
        gate = g_ref[...].astype(F32)
        merged = gate * y if merged is None else merged + gate * y
    z = jnp.dot(merged.astype(BF16), wo_ref[...], preferred_element_type=F32)
    o_ref[...] = x_ref[...] + _rms_rows(z, gpost_ref[...])


def _mix(x, plain, gates, conv_a, kt, v, lng, lnb, wa_bf, cbw, wb_bf, wx_bf, wo_bf, gpost,
         layer, *, tm):
    s, d = x.shape
    cdim = conv_a.shape[1]
    m, xdim = v.shape[1], v.shape[2]
    n_taps_b = cbw.shape[1]
    assert xdim == cdim
    assert plain.shape[1] == 4 * cdim and gates.shape[1] == N_BRANCH * d
    assert n_taps_b - 1 <= HALO_ROWS and tm % HALO_ROWS == 0
    halo_per_tile = tm // HALO_ROWS

    def layer_spec(shape):
        zeros = (0,) * len(shape)
        return pl.BlockSpec((None,) + shape, lambda i: (layer,) + zeros,
                            pipeline_mode=pl.Buffered(1))

    in_specs = [
        pl.BlockSpec((tm, cdim), lambda i: (i, 0)),
        pl.BlockSpec((tm, 4 * cdim), lambda i: (i, 0)),
        pl.BlockSpec((tm, N_BRANCH * d), lambda i: (i, 0)),
        pl.BlockSpec((HALO_ROWS, 4 * cdim),
                     lambda i: (jnp.maximum(i * halo_per_tile - 1, 0), 0)),
        pl.BlockSpec((tm, d), lambda i: (i, 0)),
        layer_spec((xdim, m)), layer_spec((m, xdim)),
        layer_spec((1, cdim)), layer_spec((1, cdim)), layer_spec((cdim, d)),
        layer_spec((n_taps_b, cdim)), layer_spec((cdim, d)), layer_spec((cdim, d)),
        layer_spec((d, d)), layer_spec((1, d)),
    ]
    weights = 3 * cdim * d * 2 + d * d * 2 + 2 * cdim * m * 2
    tiles = 2 * (tm * cdim * 4 + 4 * tm * cdim * 2 + N_BRANCH * tm * d * 2
                 + 4 * HALO_ROWS * cdim * 2 + 2 * tm * d * 4)
    scratch = (tm + HALO_ROWS) * cdim * 4 + 3 * tm * cdim * 2
    temps = 6 * tm * d * 4
    kernel = functools.partial(_mix_kernel, tm=tm, n_taps_b=n_taps_b)
    return pl.pallas_call(
        kernel,
        grid=(s // tm,),
        in_specs=in_specs,
        out_specs=pl.BlockSpec((tm, d), lambda i: (i, 0)),
        out_shape=jax.ShapeDtypeStruct((s, d), F32),
        scratch_shapes=[
            pltpu.VMEM((tm + HALO_ROWS, cdim), F32),
            pltpu.VMEM((tm, cdim), BF16),
            pltpu.VMEM((tm, cdim), BF16),
            pltpu.VMEM((tm, cdim), BF16),
        ],
        compiler_params=pltpu.CompilerParams(
            dimension_semantics=("arbitrary",),
            vmem_limit_bytes=_vmem_limit(weights + tiles + scratch + temps)),
        name="token_mix",
    )(conv_a, plain, gates, plain, x, kt, v,
      lng, lnb, wa_bf, cbw, wb_bf, wx_bf, wo_bf, gpost)


def _mlp_kernel(x_ref, gpre_ref, wup_ref, wdn_ref, gpost_ref, o_ref, h_ref, inv_ref):
    j = pl.program_id(1)

    @pl.when(j == 0)
    def _():
        def emit(rows, cols, y):
            h_ref[rows, cols] = y.astype(BF16)
            o_ref[rows, cols] = jnp.zeros(y.shape, F32)
        _rms_tile(x_ref, inv_ref, gpre_ref, emit)

    up = jnp.dot(h_ref[...], wup_ref[...].astype(BF16), preferred_element_type=F32)
    r = jnp.square(jnp.maximum(up, 0.0)).astype(BF16)
    o_ref[...] += jnp.dot(r, wdn_ref[...].astype(BF16), preferred_element_type=F32)

    @pl.when(j == pl.num_programs(1) - 1)
    def _():
        def emit(rows, cols, y):
            o_ref[rows, cols] = x_ref[rows, cols] + y
        _rms_tile(o_ref, inv_ref, gpost_ref, emit)


def _mlp(x, g_pre, w_up, w_down, g_post, layer, *, tm, tf):
    s, d = x.shape
    f = w_up.shape[2]
    est = (2 * tm * d * 4 + tm * d * 2 + 4 * d * tf * 4 + 2 * d * tf * 2 + 2 * tm * d * 4
           + tm * tf * 6)
    return pl.pallas_call(
        _mlp_kernel,
        grid=(s // tm, f // tf),
        in_specs=[
            pl.BlockSpec((tm, d), lambda i, j: (i, 0)),
            pl.BlockSpec((None, 1, d), lambda i, j: (layer, 0, 0)),
            pl.BlockSpec((None, d, tf), lambda i, j: (layer, 0, j)),
            pl.BlockSpec((None, tf, d), lambda i, j: (layer, j, 0)),
            pl.BlockSpec((None, 1, d), lambda i, j: (layer, 0, 0)),
        ],
        out_specs=pl.BlockSpec((tm, d), lambda i, j: (i, 0)),
        out_shape=jax.ShapeDtypeStruct((s, d), F32),
        scratch_shapes=[pltpu.VMEM((tm, d), BF16), pltpu.VMEM((tm, V7X_LANES), F32)],
        compiler_params=pltpu.CompilerParams(
            dimension_semantics=("parallel", "arbitrary"),
            vmem_limit_bytes=_vmem_limit(est)),
        name="mlp",
    )(x, g_pre, w_up, w_down, g_post)


def kernel(x, mem, g_mix_pre, w_in, conv_a_w, conv_a_b, ln_a_g, ln_a_b, w_a_out, conv_b_w, w_b_out, g_mem, w_kv, w_x_out, w_o, g_mix_post, g_mlp_pre, w_up, w_down, g_mlp_post):
    batch, s, d = x.shape
    assert batch == 1
    depth = w_in.shape[0]

    def row(p):
        return p.reshape(p.shape[0], 1, p.shape[1])

    bf = lambda w: w.astype(BF16)
    wa_bf, wb_bf, wx_bf, wo_bf = bf(w_a_out), bf(w_b_out), bf(w_x_out), bf(w_o)
    g_mix_pre, g_mix_post, g_mlp_pre, g_mlp_post, g_mem = map(
        row, (g_mix_pre, g_mix_post, g_mlp_pre, g_mlp_post, g_mem))
    conv_a_b, ln_a_g, ln_a_b = map(row, (conv_a_b, ln_a_g, ln_a_b))

    kt, v = _memory_kv(mem[0], g_mem, w_kv)
    xs = x[0]
    for l in range(depth):
        plain, gates, conv_a = _in_proj(xs, g_mix_pre, w_in, conv_a_w, conv_a_b, l,
                                        tm=1024, n_col_steps=8)
        xs = _mix(xs, plain, gates, conv_a, kt, v, ln_a_g, ln_a_b, wa_bf, conv_b_w,
                  wb_bf, wx_bf, wo_bf, g_mix_post, l, tm=256)
        xs = _mlp(xs, g_mlp_pre, w_up, w_down, g_mlp_post, l, tm=1024, tf=512)
    return xs[None]
```

```python
import functools

import jax
import jax.numpy as jnp
from jax import lax
from jax.experimental import pallas as pl
from jax.experimental.pallas import tpu as pltpu

F32 = jnp.float32
BF16 = jnp.bfloat16

EPS = 1e-6
XATTN_HEADS = 4
N_BRANCH = 3

V7X_VMEM_BYTES = 64 * 1024 * 1024
V7X_LANES = 128
V7X_SUBLANES = 8

HALO_ROWS = 32
NORM_CHUNK_ROWS = 32
STATS_UNROLL = 32
CONV_CHUNK_ROWS = 64
CHAINED_CONV_ROWS = 16
SHORT_CONV_ROWS = 16
LN_CHAINS = 2
CONV_CHUNK_COLS = 128


def _vmem_limit(estimate_bytes):
    return int(min(V7X_VMEM_BYTES - 4 * 1024 * 1024, estimate_bytes * 5 // 4))


def _rms_rows(x, g):
    ms = jnp.mean(x * x, axis=-1, keepdims=True)
    return x * lax.rsqrt(ms + EPS) * g


def _sigmoid(x):
    return 0.5 * jnp.tanh(0.5 * x) + 0.5


def _for_row_chunks(n_rows, chunk, body, unroll=2):
    def step(c, carry):
        body(pl.ds(pl.multiple_of(c * chunk, chunk), chunk))
        return carry
    lax.fori_loop(0, n_rows // chunk, step, 0, unroll=unroll)


def _rms_tile(src_ref, inv_ref, g_ref, emit):
    n_rows, width = src_ref.shape

    def stats(rows):
        v = src_ref[rows, :]
        inv = lax.rsqrt(jnp.mean(v * v, axis=-1, keepdims=True) + EPS)
        inv_ref[rows, :] = jnp.broadcast_to(inv, (rows.size, V7X_LANES))
    _for_row_chunks(n_rows, NORM_CHUNK_ROWS, stats, unroll=STATS_UNROLL)

    def apply(rows):
        inv = inv_ref[rows, :]
        for c0 in range(0, width, V7X_LANES):
            cols = pl.ds(c0, V7X_LANES)
            emit(rows, cols, src_ref[rows, cols] * inv * g_ref[:, cols])
    _for_row_chunks(n_rows, NORM_CHUNK_ROWS, apply)


def _zero_after(v):
    bits = lax.bitcast_convert_type(v, jnp.uint32)
    cleared = lax.shift_right_logical(lax.shift_right_logical(bits, jnp.uint32(16)),
                                      jnp.uint32(16))
    return lax.bitcast_convert_type(cleared, F32)


def _causal_conv_chunk(src_ref, w_ref, n_taps, r0, cols, emit, n_rows=None, after=None):
    n_rows = n_rows or CONV_CHUNK_ROWS
    base = HALO_ROWS - (n_taps - 1)
    ext_rows = n_rows + HALO_ROWS
    a = src_ref[pl.ds(r0, ext_rows), cols]
    if after is not None:
        a = a + jnp.concatenate([after] * (ext_rows // V7X_SUBLANES), axis=0)
    acc = None
    for phase in range(V7X_SUBLANES):
        taps = [k for k in range(n_taps) if (base + k) % V7X_SUBLANES == phase]
        if not taps:
            continue
        shifted = a if phase == 0 else pltpu.roll(a, ext_rows - phase, axis=0)
        for k in taps:
            off = (base + k) // V7X_SUBLANES * V7X_SUBLANES
            term = shifted[off:off + n_rows, :] * w_ref[pl.ds(k, 1), cols]
            acc = term if acc is None else acc + term
    emit(pl.ds(r0, n_rows), cols, acc)
    return _zero_after(acc[:V7X_SUBLANES, :])


def _kv_kernel(mem_ref, g_ref, w_ref, kt_ref, v_ref, *, xdim, scale):
    mem_n = _rms_rows(mem_ref[...], g_ref[...]).astype(BF16)
    kv = jnp.dot(mem_n, w_ref[...].astype(BF16), preferred_element_type=F32)
    kt_ref[...] = (kv[:, :xdim] * scale).T.astype(BF16)
    v_ref[...] = kv[:, xdim:].astype(BF16)


def _memory_kv(mem, g_mem, w_kv):
    depth, d, two_x = w_kv.shape
    xdim = two_x // 2
    m = mem.shape[0]
    head_dim = xdim // XATTN_HEADS
    est = (2 * (m * d * 4 + d * two_x * 4 + 2 * m * xdim * 2) + d * two_x * 2
           + 3 * m * two_x * 4)
    return pl.pallas_call(
        functools.partial(_kv_kernel, xdim=xdim, scale=head_dim ** -0.5),
        grid=(depth,),
        in_specs=[
            pl.BlockSpec((m, d), lambda l: (0, 0)),
            pl.BlockSpec((None, 1, d), lambda l: (l, 0, 0)),
            pl.BlockSpec((None, d, two_x), lambda l: (l, 0, 0)),
        ],
        out_specs=[
            pl.BlockSpec((None, xdim, m), lambda l: (l, 0, 0)),
            pl.BlockSpec((None, m, xdim), lambda l: (l, 0, 0)),
        ],
        out_shape=[
            jax.ShapeDtypeStruct((depth, xdim, m), BF16),
            jax.ShapeDtypeStruct((depth, m, xdim), BF16),
        ],
        compiler_params=pltpu.CompilerParams(
            dimension_semantics=("arbitrary",), vmem_limit_bytes=_vmem_limit(est)),
        name="memory_kv",
    )(mem, g_mem, w_kv)


def _in_proj_kernel(x_ref, g_ref, wv_ref, wg_ref, wp_ref, wgt_ref, cw_ref, cb_ref,
                    plain_ref, gates_ref, conv_ref, h_ref, inv_ref, a_ext, tail_ref, *, n_taps):
    i, j = pl.program_id(0), pl.program_id(1)
    tm, aw = x_ref.shape[0], wv_ref.shape[1]
    halo = pl.ds(0, HALO_ROWS)

    @pl.when(j == 0)
    def _():
        def emit(rows, cols, y):
            h_ref[rows, cols] = y.astype(BF16)
        _rms_tile(x_ref, inv_ref, g_ref, emit)

    @pl.when(i == 0)
    def _():
        a_ext[halo, :] = jnp.zeros((HALO_ROWS, aw), F32)

    @pl.when(i > 0)
    def _():
        a_ext[halo, :] = tail_ref[j]

    w_ag = jnp.concatenate([wv_ref[...], wg_ref[...]], axis=1).astype(BF16)
    ag = jnp.dot(h_ref[...], w_ag, preferred_element_type=F32)
    a = ag[:, :aw] * _sigmoid(ag[:, aw:])
    a_ext[pl.ds(HALO_ROWS, tm), :] = a
    tail_ref[j] = a[tm - HALO_ROWS:, :]

    slice_j = pl.ds(pl.multiple_of(j * aw, aw), aw)
    cw_j, cb_j = cw_ref.at[:, slice_j], cb_ref.at[:, slice_j]

    def emit_conv(rows, cols, acc):
        conv_ref[rows, cols] = acc + cb_j[:, cols]

    plain_ref[...] = jnp.dot(h_ref[...], wp_ref[...].astype(BF16),
                             preferred_element_type=F32).astype(plain_ref.dtype)
    gates_ref[...] = _sigmoid(jnp.dot(h_ref[...], wgt_ref[...].astype(BF16),
                                      preferred_element_type=F32)).astype(gates_ref.dtype)

    after = _zero_after(a[:V7X_SUBLANES, :])
    for r0 in range(0, tm, CHAINED_CONV_ROWS):
        after = _causal_conv_chunk(a_ext, cw_j, n_taps, r0, pl.ds(0, aw), emit_conv,
                                   n_rows=CHAINED_CONV_ROWS, after=after)


def _in_proj(x, g_pre, w_in, conv_w, conv_b, layer, *, tm, n_col_steps):
    s, d = x.shape
    n = w_in.shape[2]
    n_taps, cdim = conv_w.shape[1], conv_w.shape[2]
    n_gates = N_BRANCH * d
    n_plain = n - 2 * cdim - n_gates
    aw, pw, gw = cdim // n_col_steps, n_plain // n_col_steps, n_gates // n_col_steps
    assert aw == CONV_CHUNK_COLS and aw * n_col_steps == cdim
    assert pw % V7X_LANES == 0 and gw % V7X_LANES == 0
    assert pw * n_col_steps == n_plain and gw * n_col_steps == n_gates
    assert (2 * cdim) % pw == 0 and (2 * cdim + n_plain) % gw == 0
    gate_first = cdim // aw
    plain_first = (2 * cdim) // pw
    gates_first = (2 * cdim + n_plain) // gw
    cols = 2 * aw + pw + gw
    est = (2 * tm * d * 4 + tm * d * 2 + tm * V7X_LANES * 4
           + 2 * d * cols * 4 + d * cols * 2
           + 2 * tm * (pw + gw) * 2 + 2 * tm * aw * 4
           + (tm + HALO_ROWS) * aw * 4 + n_col_steps * HALO_ROWS * aw * 4 + 2 * tm * aw * 4)
    return pl.pallas_call(
        functools.partial(_in_proj_kernel, n_taps=n_taps),
        grid=(s // tm, n_col_steps),
        in_specs=[
            pl.BlockSpec((tm, d), lambda i, j: (i, 0)),
            pl.BlockSpec((None, 1, d), lambda i, j: (layer, 0, 0)),
            pl.BlockSpec((None, d, aw), lambda i, j: (layer, 0, j)),
            pl.BlockSpec((None, d, aw), lambda i, j: (layer, 0, gate_first + j)),
            pl.BlockSpec((None, d, pw), lambda i, j: (layer, 0, plain_first + j)),
            pl.BlockSpec((None, d, gw), lambda i, j: (layer, 0, gates_first + j)),
            pl.BlockSpec((None, n_taps, cdim), lambda i, j: (layer, 0, 0)),
            pl.BlockSpec((None, 1, cdim), lambda i, j: (layer, 0, 0)),
        ],
        out_specs=[
            pl.BlockSpec((tm, pw), lambda i, j: (i, j)),
            pl.BlockSpec((tm, gw), lambda i, j: (i, j)),
            pl.BlockSpec((tm, aw), lambda i, j: (i, j)),
        ],
        out_shape=[
            jax.ShapeDtypeStruct((s, n_plain), BF16),
            jax.ShapeDtypeStruct((s, n_gates), BF16),
            jax.ShapeDtypeStruct((s, cdim), F32),
        ],
        scratch_shapes=[
            pltpu.VMEM((tm, d), BF16),
            pltpu.VMEM((tm, V7X_LANES), F32),
            pltpu.VMEM((tm + HALO_ROWS, aw), F32),
            pltpu.VMEM((n_col_steps, HALO_ROWS, aw), F32),
        ],
        compiler_params=pltpu.CompilerParams(
            dimension_semantics=("arbitrary", "arbitrary"),
            vmem_limit_bytes=_vmem_limit(est)),
        name="in_proj",
    )(x, g_pre, w_in, w_in, w_in, w_in, conv_w, conv_b)


def _mix_kernel(ca_ref, plain_ref, gates_ref, halo_ref, x_ref, kt_ref, v_ref,
                lng_ref, lnb_ref, wa_ref, cbw_ref, wb_ref, wx_ref, wo_ref, gpost_ref, o_ref,
                p_ext, act_a, act_b, act_x, *, tm, n_taps_b):
    not_first = pl.program_id(0) > 0
    cdim, d = ca_ref.shape[1], x_ref.shape[1]
    sb_ref, sc_ref, sx_ref, q_ref = (plain_ref.at[:, pl.ds(c * cdim, cdim)] for c in range(4))
    hsc_ref, hsx_ref = (halo_ref.at[:, pl.ds(c * cdim, cdim)] for c in (1, 2))
    g0_ref, g1_ref, g2_ref = (gates_ref.at[:, pl.ds(b * d, d)] for b in range(N_BRANCH))

    hd = kt_ref.shape[0] // XATTN_HEADS
    for h in range(XATTN_HEADS):
        hs = pl.ds(h * hd, hd)
        s = jnp.dot(q_ref[:, hs], kt_ref[hs, :], preferred_element_type=F32)
        e = jnp.exp(s - jnp.max(s, axis=-1, keepdims=True))
        p = e * (1.0 / jnp.sum(e, axis=-1, keepdims=True))
        act_x[:, hs] = jnp.dot(p.astype(BF16), v_ref[:, hs],
                               preferred_element_type=F32).astype(BF16)
    y_x = jnp.dot(act_x[...], wx_ref[...], preferred_element_type=F32)

    p_halo = hsc_ref[...].astype(F32) * hsx_ref[...].astype(F32)
    p_ext[pl.ds(0, HALO_ROWS), :] = jnp.where(not_first, p_halo, 0.0)
    for r0 in range(0, tm, NORM_CHUNK_ROWS):
        rows = pl.ds(r0, NORM_CHUNK_ROWS)
        p_ext[pl.ds(HALO_ROWS + r0, NORM_CHUNK_ROWS), :] = (
            sc_ref[rows, :].astype(F32) * sx_ref[rows, :].astype(F32))

    def emit_b(rows, cols, acc):
        act_b[rows, cols] = (sb_ref[rows, cols].astype(F32) * acc).astype(BF16)
    after = None
    for r0 in range(0, tm, SHORT_CONV_ROWS):
        for c0 in range(0, cdim, CONV_CHUNK_COLS):
            after = _causal_conv_chunk(p_ext, cbw_ref, n_taps_b, r0, pl.ds(c0, CONV_CHUNK_COLS),
                                       emit_b, n_rows=SHORT_CONV_ROWS, after=after)
    y_b = jnp.dot(act_b[...], wb_ref[...], preferred_element_type=F32)

    chains = [after] * LN_CHAINS
    for n, r0 in enumerate(range(0, tm, NORM_CHUNK_ROWS)):
        rows = pl.ds(r0, NORM_CHUNK_ROWS)
        zeros = jnp.concatenate([chains[n % LN_CHAINS]] * (NORM_CHUNK_ROWS // V7X_SUBLANES), axis=0)
        c = jnp.concatenate([ca_ref[rows, :V7X_LANES] + zeros, ca_ref[rows, V7X_LANES:]], axis=1)
        mu = jnp.mean(c, axis=-1, keepdims=True)
        cen = c - mu
        var = jnp.mean(cen * cen, axis=-1, keepdims=True)
        y = cen * lax.rsqrt(var + EPS) * lng_ref[...] + lnb_ref[...]
        act_a[rows, :] = (y * _sigmoid(y)).astype(BF16)
        chains[n % LN_CHAINS] = _zero_after(y[:V7X_SUBLANES, :V7X_LANES])
    y_a = jnp.dot(act_a[...], wa_ref[...], preferred_element_type=F32)

    merged = None
    for y, g_ref in ((y_a, g0_ref), (y_b, g1_ref), (y_x, g2_ref)):
        gate = g_ref[...].astype(F32)
        merged = gate * y if merged is None else merged + gate * y
    z = jnp.dot(merged.astype(BF16), wo_ref[...], preferred_element_type=F32)
    o_ref[...] = x_ref[...] + _rms_rows(z, gpost_ref[...])


def _mix(x, plain, gates, conv_a, kt, v, lng, lnb, wa_bf, cbw, wb_bf, wx_bf, wo_bf, gpost,
         layer, *, tm):
    s, d = x.shape
    cdim = conv_a.shape[1]
    m, xdim = v.shape[1], v.shape[2]
    n_taps_b = cbw.shape[1]
    assert xdim == cdim
    assert plain.shape[1] == 4 * cdim and gates.shape[1] == N_BRANCH * d
    assert n_taps_b - 1 <= HALO_ROWS and tm % HALO_ROWS == 0
    halo_per_tile = tm // HALO_ROWS

    def layer_spec(shape):
        zeros = (0,) * len(shape)
        return pl.BlockSpec((None,) + shape, lambda i: (layer,) + zeros,
                            pipeline_mode=pl.Buffered(1))

    in_specs = [
        pl.BlockSpec((tm, cdim), lambda i: (i, 0)),
        pl.BlockSpec((tm, 4 * cdim), lambda i: (i, 0)),
        pl.BlockSpec((tm, N_BRANCH * d), lambda i: (i, 0)),
        pl.BlockSpec((HALO_ROWS, 4 * cdim),
                     lambda i: (jnp.maximum(i * halo_per_tile - 1, 0), 0)),
        pl.BlockSpec((tm, d), lambda i: (i, 0)),
        layer_spec((xdim, m)), layer_spec((m, xdim)),
        layer_spec((1, cdim)), layer_spec((1, cdim)), layer_spec((cdim, d)),
        layer_spec((n_taps_b, cdim)), layer_spec((cdim, d)), layer_spec((cdim, d)),
        layer_spec((d, d)), layer_spec((1, d)),
    ]
    weights = 3 * cdim * d * 2 + d * d * 2 + 2 * cdim * m * 2
    tiles = 2 * (tm * cdim * 4 + 4 * tm * cdim * 2 + N_BRANCH * tm * d * 2
                 + 4 * HALO_ROWS * cdim * 2 + 2 * tm * d * 4)
    scratch = (tm + HALO_ROWS) * cdim * 4 + 3 * tm * cdim * 2
    temps = 6 * tm * d * 4
    kernel = functools.partial(_mix_kernel, tm=tm, n_taps_b=n_taps_b)
    return pl.pallas_call(
        kernel,
        grid=(s // tm,),
        in_specs=in_specs,
        out_specs=pl.BlockSpec((tm, d), lambda i: (i, 0)),
        out_shape=jax.ShapeDtypeStruct((s, d), F32),
        scratch_shapes=[
            pltpu.VMEM((tm + HALO_ROWS, cdim), F32),
            pltpu.VMEM((tm, cdim), BF16),
            pltpu.VMEM((tm, cdim), BF16),
            pltpu.VMEM((tm, cdim), BF16),
        ],
        compiler_params=pltpu.CompilerParams(
            dimension_semantics=("arbitrary",),
            vmem_limit_bytes=_vmem_limit(weights + tiles + scratch + temps)),
        name="token_mix",
    )(conv_a, plain, gates, plain, x, kt, v,
      lng, lnb, wa_bf, cbw, wb_bf, wx_bf, wo_bf, gpost)


def _mlp_kernel(x_ref, gpre_ref, wup_ref, wdn_ref, gpost_ref, o_ref, h_ref, inv_ref):
    j = pl.program_id(1)

    @pl.when(j == 0)
    def _():
        def emit(rows, cols, y):
            h_ref[rows, cols] = y.astype(BF16)
            o_ref[rows, cols] = jnp.zeros(y.shape, F32)
        _rms_tile(x_ref, inv_ref, gpre_ref, emit)

    up = jnp.dot(h_ref[...], wup_ref[...].astype(BF16), preferred_element_type=F32)
    r = jnp.square(jnp.maximum(up, 0.0)).astype(BF16)
    o_ref[...] += jnp.dot(r, wdn_ref[...].astype(BF16), preferred_element_type=F32)

    @pl.when(j == pl.num_programs(1) - 1)
    def _():
        def emit(rows, cols, y):
            o_ref[rows, cols] = x_ref[rows, cols] + y
        _rms_tile(o_ref, inv_ref, gpost_ref, emit)


def _mlp(x, g_pre, w_up, w_down, g_post, layer, *, tm, tf):
    s, d = x.shape
    f = w_up.shape[2]
    est = (2 * tm * d * 4 + tm * d * 2 + 4 * d * tf * 4 + 2 * d * tf * 2 + 2 * tm * d * 4
           + tm * tf * 6)
    return pl.pallas_call(
        _mlp_kernel,
        grid=(s // tm, f // tf),
        in_specs=[
            pl.BlockSpec((tm, d), lambda i, j: (i, 0)),
            pl.BlockSpec((None, 1, d), lambda i, j: (layer, 0, 0)),
            pl.BlockSpec((None, d, tf), lambda i, j: (layer, 0, j)),
            pl.BlockSpec((None, tf, d), lambda i, j: (layer, j, 0)),
            pl.BlockSpec((None, 1, d), lambda i, j: (layer, 0, 0)),
        ],
        out_specs=pl.BlockSpec((tm, d), lambda i, j: (i, 0)),
        out_shape=jax.ShapeDtypeStruct((s, d), F32),
        scratch_shapes=[pltpu.VMEM((tm, d), BF16), pltpu.VMEM((tm, V7X_LANES), F32)],
        compiler_params=pltpu.CompilerParams(
            dimension_semantics=("parallel", "arbitrary"),
            vmem_limit_bytes=_vmem_limit(est)),
        name="mlp",
    )(x, g_pre, w_up, w_down, g_post)


def kernel(x, mem, g_mix_pre, w_in, conv_a_w, conv_a_b, ln_a_g, ln_a_b, w_a_out, conv_b_w, w_b_out, g_mem, w_kv, w_x_out, w_o, g_mix_post, g_mlp_pre, w_up, w_down, g_mlp_post):
    batch, s, d = x.shape
    assert batch == 1
    depth = w_in.shape[0]

    def row(p):
        return p.reshape(p.shape[0], 1, p.shape[1])

    bf = lambda w: w.astype(BF16)
    wa_bf, wb_bf, wx_bf, wo_bf = bf(w_a_out), bf(w_b_out), bf(w_x_out), bf(w_o)
    g_mix_pre, g_mix_post, g_mlp_pre, g_mlp_post, g_mem = map(
        row, (g_mix_pre, g_mix_post, g_mlp_pre, g_mlp_post, g_mem))
    conv_a_b, ln_a_g, ln_a_b = map(row, (conv_a_b, ln_a_g, ln_a_b))

    kt, v = _memory_kv(mem[0], g_mem, w_kv)
    xs = x[0]
    for l in range(depth):
        plain, gates, conv_a = _in_proj(xs, g_mix_pre, w_in, conv_a_w, conv_a_b, l,
                                        tm=1024, n_col_steps=8)
        xs = _mix(xs, plain, gates, conv_a, kt, v, ln_a_g, ln_a_b, wa_bf, conv_b_w,
                  wb_bf, wx_bf, wo_bf, g_mix_post, l, tm=256)
        xs = _mlp(xs, g_mlp_pre, w_up, w_down, g_mlp_post, l, tm=1024, tf=512)
    return xs[None]
```

```python
import functools

import jax
import jax.numpy as jnp
from jax import lax
from jax.experimental import pallas as pl
from jax.experimental.pallas import tpu as pltpu

F32 = jnp.float32
BF16 = jnp.bfloat16

EPS = 1e-6
XATTN_HEADS = 4
N_BRANCH = 3

V7X_VMEM_BYTES = 64 * 1024 * 1024
V7X_LANES = 128
V7X_SUBLANES = 8

HALO_ROWS = 32
NORM_CHUNK_ROWS = 32
STATS_UNROLL = 32
CONV_CHUNK_ROWS = 64
CHAINED_CONV_ROWS = 16
SHORT_CONV_ROWS = 16
WEIGHT_STAGE_ROWS = 256
LN_CHAINS = 2
CONV_CHUNK_COLS = 128


def _vmem_limit(estimate_bytes):
    return int(min(V7X_VMEM_BYTES - 4 * 1024 * 1024, estimate_bytes * 5 // 4))


def _rms_rows(x, g):
    ms = jnp.mean(x * x, axis=-1, keepdims=True)
    return x * lax.rsqrt(ms + EPS) * g


def _sigmoid(x):
    return 0.5 * jnp.tanh(0.5 * x) + 0.5


def _for_row_chunks(n_rows, chunk, body, unroll=2):
    def step(c, carry):
        body(pl.ds(pl.multiple_of(c * chunk, chunk), chunk))
        return carry
    lax.fori_loop(0, n_rows // chunk, step, 0, unroll=unroll)


def _rms_tile(src_ref, inv_ref, g_ref, emit):
    n_rows, width = src_ref.shape

    def stats(rows):
        v = src_ref[rows, :]
        inv = lax.rsqrt(jnp.mean(v * v, axis=-1, keepdims=True) + EPS)
        inv_ref[rows, :] = jnp.broadcast_to(inv, (rows.size, V7X_LANES))
    _for_row_chunks(n_rows, NORM_CHUNK_ROWS, stats, unroll=STATS_UNROLL)

    def apply(rows):
        inv = inv_ref[rows, :]
        for c0 in range(0, width, V7X_LANES):
            cols = pl.ds(c0, V7X_LANES)
            emit(rows, cols, src_ref[rows, cols] * inv * g_ref[:, cols])
    _for_row_chunks(n_rows, NORM_CHUNK_ROWS, apply)


def _zero_after(v):
    bits = lax.bitcast_convert_type(v, jnp.uint32)
    cleared = lax.shift_right_logical(lax.shift_right_logical(bits, jnp.uint32(16)),
                                      jnp.uint32(16))
    return lax.bitcast_convert_type(cleared, F32)


def _causal_conv_chunk(src_ref, w_ref, n_taps, r0, cols, emit, n_rows=None, after=None):
    n_rows = n_rows or CONV_CHUNK_ROWS
    base = HALO_ROWS - (n_taps - 1)
    ext_rows = n_rows + HALO_ROWS
    a = src_ref[pl.ds(r0, ext_rows), cols]
    if after is not None:
        a = a + jnp.concatenate([after] * (ext_rows // V7X_SUBLANES), axis=0)
    acc = None
    for phase in range(V7X_SUBLANES):
        taps = [k for k in range(n_taps) if (base + k) % V7X_SUBLANES == phase]
        if not taps:
            continue
        shifted = a if phase == 0 else pltpu.roll(a, ext_rows - phase, axis=0)
        for k in taps:
            off = (base + k) // V7X_SUBLANES * V7X_SUBLANES
            term = shifted[off:off + n_rows, :] * w_ref[pl.ds(k, 1), cols]
            acc = term if acc is None else acc + term
    emit(pl.ds(r0, n_rows), cols, acc)
    return _zero_after(acc[:V7X_SUBLANES, :])


def _kv_kernel(mem_ref, g_ref, w_ref, kt_ref, v_ref, *, xdim, scale):
    mem_n = _rms_rows(mem_ref[...], g_ref[...]).astype(BF16)
    kv = jnp.dot(mem_n, w_ref[...].astype(BF16), preferred_element_type=F32)
    kt_ref[...] = (kv[:, :xdim] * scale).T.astype(BF16)
    v_ref[...] = kv[:, xdim:].astype(BF16)


def _memory_kv(mem, g_mem, w_kv):
    depth, d, two_x = w_kv.shape
    xdim = two_x // 2
    m = mem.shape[0]
    head_dim = xdim // XATTN_HEADS
    est = (2 * (m * d * 4 + d * two_x * 4 + 2 * m * xdim * 2) + d * two_x * 2
           + 3 * m * two_x * 4)
    return pl.pallas_call(
        functools.partial(_kv_kernel, xdim=xdim, scale=head_dim ** -0.5),
        grid=(depth,),
        in_specs=[
            pl.BlockSpec((m, d), lambda l: (0, 0)),
            pl.BlockSpec((None, 1, d), lambda l: (l, 0, 0)),
            pl.BlockSpec((None, d, two_x), lambda l: (l, 0, 0)),
        ],
        out_specs=[
            pl.BlockSpec((None, xdim, m), lambda l: (l, 0, 0)),
            pl.BlockSpec((None, m, xdim), lambda l: (l, 0, 0)),
        ],
        out_shape=[
            jax.ShapeDtypeStruct((depth, xdim, m), BF16),
            jax.ShapeDtypeStruct((depth, m, xdim), BF16),
        ],
        compiler_params=pltpu.CompilerParams(
            dimension_semantics=("arbitrary",), vmem_limit_bytes=_vmem_limit(est)),
        name="memory_kv",
    )(mem, g_mem, w_kv)


def _in_proj_kernel(x_ref, g_ref, wv_ref, wg_ref, wp_ref, wgt_ref, cw_ref, cb_ref,
                    plain_ref, gates_ref, conv_ref, h_ref, inv_ref, a_ext, tail_ref, *, n_taps):
    i, j = pl.program_id(0), pl.program_id(1)
    tm, aw = x_ref.shape[0], wv_ref.shape[1]
    halo = pl.ds(0, HALO_ROWS)

    @pl.when(j == 0)
    def _():
        def emit(rows, cols, y):
            h_ref[rows, cols] = y.astype(BF16)
        _rms_tile(x_ref, inv_ref, g_ref, emit)

    @pl.when(i == 0)
    def _():
        a_ext[halo, :] = jnp.zeros((HALO_ROWS, aw), F32)

    @pl.when(i > 0)
    def _():
        a_ext[halo, :] = tail_ref[j]

    w_ag = jnp.concatenate([wv_ref[...], wg_ref[...]], axis=1).astype(BF16)
    ag = jnp.dot(h_ref[...], w_ag, preferred_element_type=F32)
    a = ag[:, :aw] * _sigmoid(ag[:, aw:])
    a_ext[pl.ds(HALO_ROWS, tm), :] = a
    tail_ref[j] = a[tm - HALO_ROWS:, :]

    slice_j = pl.ds(pl.multiple_of(j * aw, aw), aw)
    cw_j, cb_j = cw_ref.at[:, slice_j], cb_ref.at[:, slice_j]

    def emit_conv(rows, cols, acc):
        conv_ref[rows, cols] = acc + cb_j[:, cols]

    plain_ref[...] = jnp.dot(h_ref[...], wp_ref[...].astype(BF16),
                             preferred_element_type=F32).astype(plain_ref.dtype)
    gates_ref[...] = _sigmoid(jnp.dot(h_ref[...], wgt_ref[...].astype(BF16),
                                      preferred_element_type=F32)).astype(gates_ref.dtype)

    after = _zero_after(a[:V7X_SUBLANES, :])
    for r0 in range(0, tm, CHAINED_CONV_ROWS):
        after = _causal_conv_chunk(a_ext, cw_j, n_taps, r0, pl.ds(0, aw), emit_conv,
                                   n_rows=CHAINED_CONV_ROWS, after=after)


def _in_proj(x, g_pre, w_in, conv_w, conv_b, layer, *, tm, n_col_steps):
    s, d = x.shape
    n = w_in.shape[2]
    n_taps, cdim = conv_w.shape[1], conv_w.shape[2]
    n_gates = N_BRANCH * d
    n_plain = n - 2 * cdim - n_gates
    aw, pw, gw = cdim // n_col_steps, n_plain // n_col_steps, n_gates // n_col_steps
    assert aw == CONV_CHUNK_COLS and aw * n_col_steps == cdim
    assert pw % V7X_LANES == 0 and gw % V7X_LANES == 0
    assert pw * n_col_steps == n_plain and gw * n_col_steps == n_gates
    assert (2 * cdim) % pw == 0 and (2 * cdim + n_plain) % gw == 0
    gate_first = cdim // aw
    plain_first = (2 * cdim) // pw
    gates_first = (2 * cdim + n_plain) // gw
    cols = 2 * aw + pw + gw
    est = (2 * tm * d * 4 + tm * d * 2 + tm * V7X_LANES * 4
           + 2 * d * cols * 4 + d * cols * 2
           + 2 * tm * (pw + gw) * 2 + 2 * tm * aw * 4
           + (tm + HALO_ROWS) * aw * 4 + n_col_steps * HALO_ROWS * aw * 4 + 2 * tm * aw * 4)
    return pl.pallas_call(
        functools.partial(_in_proj_kernel, n_taps=n_taps),
        grid=(s // tm, n_col_steps),
        in_specs=[
            pl.BlockSpec((tm, d), lambda i, j: (i, 0)),
            pl.BlockSpec((None, 1, d), lambda i, j: (layer, 0, 0)),
            pl.BlockSpec((None, d, aw), lambda i, j: (layer, 0, j)),
            pl.BlockSpec((None, d, aw), lambda i, j: (layer, 0, gate_first + j)),
            pl.BlockSpec((None, d, pw), lambda i, j: (layer, 0, plain_first + j)),
            pl.BlockSpec((None, d, gw), lambda i, j: (layer, 0, gates_first + j)),
            pl.BlockSpec((None, n_taps, cdim), lambda i, j: (layer, 0, 0)),
            pl.BlockSpec((None, 1, cdim), lambda i, j: (layer, 0, 0)),
        ],
        out_specs=[
            pl.BlockSpec((tm, pw), lambda i, j: (i, j)),
            pl.BlockSpec((tm, gw), lambda i, j: (i, j)),
            pl.BlockSpec((tm, aw), lambda i, j: (i, j)),
        ],
        out_shape=[
            jax.ShapeDtypeStruct((s, n_plain), BF16),
            jax.ShapeDtypeStruct((s, n_gates), BF16),
            jax.ShapeDtypeStruct((s, cdim), F32),
        ],
        scratch_shapes=[
            pltpu.VMEM((tm, d), BF16),
            pltpu.VMEM((tm, V7X_LANES), F32),
            pltpu.VMEM((tm + HALO_ROWS, aw), F32),
            pltpu.VMEM((n_col_steps, HALO_ROWS, aw), F32),
        ],
        compiler_params=pltpu.CompilerParams(
            dimension_semantics=("arbitrary", "arbitrary"),
            vmem_limit_bytes=_vmem_limit(est)),
        name="in_proj",
    )(x, g_pre, w_in, w_in, w_in, w_in, conv_w, conv_b)


def _stage_weights(layer, pairs, stage, sems):
    chunks = [(w_hbm, dst, r0) for w_hbm, dst in pairs
              for r0 in range(0, dst.shape[0], WEIGHT_STAGE_ROWS)]

    def copy(k):
        w_hbm, _, r0 = chunks[k]
        return pltpu.make_async_copy(w_hbm.at[layer, pl.ds(r0, WEIGHT_STAGE_ROWS), :],
                                     stage.at[k % 2], sems.at[k % 2])

    copy(0).start()
    for k, (_, dst, r0) in enumerate(chunks):
        if k + 1 < len(chunks):
            copy(k + 1).start()
        copy(k).wait()
        dst[pl.ds(r0, WEIGHT_STAGE_ROWS), :] = stage[k % 2].astype(BF16)


def _mix_kernel(ca_ref, plain_ref, gates_ref, halo_ref, x_ref, kt_ref, v_ref,
                lng_ref, lnb_ref, cbw_ref, gpost_ref, wa_hbm, wb_hbm, wx_hbm, wo_hbm, o_ref,
                p_ext, act_a, act_b, act_x, wa_ref, wb_ref, wx_ref, wo_ref, stage, sems,
                *, tm, n_taps_b, layer):
    @pl.when(pl.program_id(0) == 0)
    def _():
        _stage_weights(layer, ((wx_hbm, wx_ref), (wb_hbm, wb_ref), (wa_hbm, wa_ref),
                               (wo_hbm, wo_ref)), stage, sems)

    not_first = pl.program_id(0) > 0
    cdim, d = ca_ref.shape[1], x_ref.shape[1]
    sb_ref, sc_ref, sx_ref, q_ref = (plain_ref.at[:, pl.ds(c * cdim, cdim)] for c in range(4))
    hsc_ref, hsx_ref = (halo_ref.at[:, pl.ds(c * cdim, cdim)] for c in (1, 2))
    g0_ref, g1_ref, g2_ref = (gates_ref.at[:, pl.ds(b * d, d)] for b in range(N_BRANCH))

    hd = kt_ref.shape[0] // XATTN_HEADS
    for h in range(XATTN_HEADS):
        hs = pl.ds(h * hd, hd)
        s = jnp.dot(q_ref[:, hs], kt_ref[hs, :], preferred_element_type=F32)
        e = jnp.exp(s - jnp.max(s, axis=-1, keepdims=True))
        p = e * (1.0 / jnp.sum(e, axis=-1, keepdims=True))
        act_x[:, hs] = jnp.dot(p.astype(BF16), v_ref[:, hs],
                               preferred_element_type=F32).astype(BF16)
    y_x = jnp.dot(act_x[...], wx_ref[...], preferred_element_type=F32)

    p_halo = hsc_ref[...].astype(F32) * hsx_ref[...].astype(F32)
    p_ext[pl.ds(0, HALO_ROWS), :] = jnp.where(not_first, p_halo, 0.0)
    for r0 in range(0, tm, NORM_CHUNK_ROWS):
        rows = pl.ds(r0, NORM_CHUNK_ROWS)
        p_ext[pl.ds(HALO_ROWS + r0, NORM_CHUNK_ROWS), :] = (
            sc_ref[rows, :].astype(F32) * sx_ref[rows, :].astype(F32))

    def emit_b(rows, cols, acc):
        act_b[rows, cols] = (sb_ref[rows, cols].astype(F32) * acc).astype(BF16)
    after = None
    for r0 in range(0, tm, SHORT_CONV_ROWS):
        for c0 in range(0, cdim, CONV_CHUNK_COLS):
            after = _causal_conv_chunk(p_ext, cbw_ref, n_taps_b, r0, pl.ds(c0, CONV_CHUNK_COLS),
                                       emit_b, n_rows=SHORT_CONV_ROWS, after=after)
    y_b = jnp.dot(act_b[...], wb_ref[...], preferred_element_type=F32)

    chains = [after] * LN_CHAINS
    for n, r0 in enumerate(range(0, tm, NORM_CHUNK_ROWS)):
        rows = pl.ds(r0, NORM_CHUNK_ROWS)
        zeros = jnp.concatenate([chains[n % LN_CHAINS]] * (NORM_CHUNK_ROWS // V7X_SUBLANES), axis=0)
        c = jnp.concatenate([ca_ref[rows, :V7X_LANES] + zeros, ca_ref[rows, V7X_LANES:]], axis=1)
        mu = jnp.mean(c, axis=-1, keepdims=True)
        cen = c - mu
        var = jnp.mean(cen * cen, axis=-1, keepdims=True)
        y = cen * lax.rsqrt(var + EPS) * lng_ref[...] + lnb_ref[...]
        act_a[rows, :] = (y * _sigmoid(y)).astype(BF16)
        chains[n % LN_CHAINS] = _zero_after(y[:V7X_SUBLANES, :V7X_LANES])
    y_a = jnp.dot(act_a[...], wa_ref[...], preferred_element_type=F32)

    merged = None
    for y, g_ref in ((y_a, g0_ref), (y_b, g1_ref), (y_x, g2_ref)):
        gate = g_ref[...].astype(F32)
        merged = gate * y if merged is None else merged + gate * y
    z = jnp.dot(merged.astype(BF16), wo_ref[...], preferred_element_type=F32)
    o_ref[...] = x_ref[...] + _rms_rows(z, gpost_ref[...])


def _mix(x, plain, gates, conv_a, kt, v, lng, lnb, w_a, cbw, w_b, w_x, w_o, gpost,
         layer, *, tm):
    s, d = x.shape
    cdim = conv_a.shape[1]
    m, xdim = v.shape[1], v.shape[2]
    n_taps_b = cbw.shape[1]
    assert xdim == cdim
    assert plain.shape[1] == 4 * cdim and gates.shape[1] == N_BRANCH * d
    assert n_taps_b - 1 <= HALO_ROWS and tm % HALO_ROWS == 0
    halo_per_tile = tm // HALO_ROWS

    def layer_spec(shape):
        zeros = (0,) * len(shape)
        return pl.BlockSpec((None,) + shape, lambda i: (layer,) + zeros,
                            pipeline_mode=pl.Buffered(1))

    in_specs = [
        pl.BlockSpec((tm, cdim), lambda i: (i, 0)),
        pl.BlockSpec((tm, 4 * cdim), lambda i: (i, 0)),
        pl.BlockSpec((tm, N_BRANCH * d), lambda i: (i, 0)),
        pl.BlockSpec((HALO_ROWS, 4 * cdim),
                     lambda i: (jnp.maximum(i * halo_per_tile - 1, 0), 0)),
        pl.BlockSpec((tm, d), lambda i: (i, 0)),
        layer_spec((xdim, m)), layer_spec((m, xdim)),
        layer_spec((1, cdim)), layer_spec((1, cdim)),
        layer_spec((n_taps_b, cdim)), layer_spec((1, d)),
    ] + [pl.BlockSpec(memory_space=pl.ANY)] * 4
    assert cdim % WEIGHT_STAGE_ROWS == 0 and d % WEIGHT_STAGE_ROWS == 0
    weights = 3 * cdim * d * 2 + d * d * 2 + 2 * cdim * m * 2
    tiles = 2 * (tm * cdim * 4 + 4 * tm * cdim * 2 + N_BRANCH * tm * d * 2
                 + 4 * HALO_ROWS * cdim * 2 + 2 * tm * d * 4)
    scratch = (tm + HALO_ROWS) * cdim * 4 + 3 * tm * cdim * 2 + 2 * WEIGHT_STAGE_ROWS * d * 4
    temps = 6 * tm * d * 4
    kernel = functools.partial(_mix_kernel, tm=tm, n_taps_b=n_taps_b, layer=layer)
    return pl.pallas_call(
        kernel,
        grid=(s // tm,),
        in_specs=in_specs,
        out_specs=pl.BlockSpec((tm, d), lambda i: (i, 0)),
        out_shape=jax.ShapeDtypeStruct((s, d), F32),
        scratch_shapes=[
            pltpu.VMEM((tm + HALO_ROWS, cdim), F32),
            pltpu.VMEM((tm, cdim), BF16),
            pltpu.VMEM((tm, cdim), BF16),
            pltpu.VMEM((tm, cdim), BF16),
            pltpu.VMEM((cdim, d), BF16),
            pltpu.VMEM((cdim, d), BF16),
            pltpu.VMEM((cdim, d), BF16),
            pltpu.VMEM((d, d), BF16),
            pltpu.VMEM((2, WEIGHT_STAGE_ROWS, d), F32),
            pltpu.SemaphoreType.DMA((2,)),
        ],
        compiler_params=pltpu.CompilerParams(
            dimension_semantics=("arbitrary",),
            vmem_limit_bytes=_vmem_limit(weights + tiles + scratch + temps)),
        name="token_mix",
    )(conv_a, plain, gates, plain, x, kt, v, lng, lnb, cbw, gpost, w_a, w_b, w_x, w_o)


def _mlp_kernel(x_ref, gpre_ref, wup_ref, wdn_ref, gpost_ref, o_ref, h_ref, inv_ref):
    j = pl.program_id(1)

    @pl.when(j == 0)
    def _():
        def emit(rows, cols, y):
            h_ref[rows, cols] = y.astype(BF16)
            o_ref[rows, cols] = jnp.zeros(y.shape, F32)
        _rms_tile(x_ref, inv_ref, gpre_ref, emit)

    up = jnp.dot(h_ref[...], wup_ref[...].astype(BF16), preferred_element_type=F32)
    r = jnp.square(jnp.maximum(up, 0.0)).astype(BF16)
    o_ref[...] += jnp.dot(r, wdn_ref[...].astype(BF16), preferred_element_type=F32)

    @pl.when(j == pl.num_programs(1) - 1)
    def _():
        def emit(rows, cols, y):
            o_ref[rows, cols] = x_ref[rows, cols] + y
        _rms_tile(o_ref, inv_ref, gpost_ref, emit)


def _mlp(x, g_pre, w_up, w_down, g_post, layer, *, tm, tf):
    s, d = x.shape
    f = w_up.shape[2]
    est = (2 * tm * d * 4 + tm * d * 2 + 4 * d * tf * 4 + 2 * d * tf * 2 + 2 * tm * d * 4
           + tm * tf * 6)
    return pl.pallas_call(
        _mlp_kernel,
        grid=(s // tm, f // tf),
        in_specs=[
            pl.BlockSpec((tm, d), lambda i, j: (i, 0)),
            pl.BlockSpec((None, 1, d), lambda i, j: (layer, 0, 0)),
            pl.BlockSpec((None, d, tf), lambda i, j: (layer, 0, j)),
            pl.BlockSpec((None, tf, d), lambda i, j: (layer, j, 0)),
            pl.BlockSpec((None, 1, d), lambda i, j: (layer, 0, 0)),
        ],
        out_specs=pl.BlockSpec((tm, d), lambda i, j: (i, 0)),
        out_shape=jax.ShapeDtypeStruct((s, d), F32),
        scratch_shapes=[pltpu.VMEM((tm, d), BF16), pltpu.VMEM((tm, V7X_LANES), F32)],
        compiler_params=pltpu.CompilerParams(
            dimension_semantics=("parallel", "arbitrary"),
            vmem_limit_bytes=_vmem_limit(est)),
        name="mlp",
    )(x, g_pre, w_up, w_down, g_post)


def kernel(x, mem, g_mix_pre, w_in, conv_a_w, conv_a_b, ln_a_g, ln_a_b, w_a_out, conv_b_w, w_b_out, g_mem, w_kv, w_x_out, w_o, g_mix_post, g_mlp_pre, w_up, w_down, g_mlp_post):
    batch, s, d = x.shape
    assert batch == 1
    depth = w_in.shape[0]

    def row(p):
        return p.reshape(p.shape[0], 1, p.shape[1])

    g_mix_pre, g_mix_post, g_mlp_pre, g_mlp_post, g_mem = map(
        row, (g_mix_pre, g_mix_post, g_mlp_pre, g_mlp_post, g_mem))
    conv_a_b, ln_a_g, ln_a_b = map(row, (conv_a_b, ln_a_g, ln_a_b))

    kt, v = _memory_kv(mem[0], g_mem, w_kv)
    xs = x[0]
    for l in range(depth):
        plain, gates, conv_a = _in_proj(xs, g_mix_pre, w_in, conv_a_w, conv_a_b, l,
                                        tm=1024, n_col_steps=8)
        xs = _mix(xs, plain, gates, conv_a, kt, v, ln_a_g, ln_a_b, w_a_out, conv_b_w,
                  w_b_out, w_x_out, w_o, g_mix_post, l, tm=256)
        xs = _mlp(xs, g_mlp_pre, w_up, w_down, g_mlp_post, l, tm=1024, tf=512)
    return xs[None]
```
